```python
import functools
import jax, jax.numpy as jnp
from jax import lax
import numpy as np

D_MODEL = 2048
BATCH = 8
SEQ = 2048
DEPTH = 1
DEC_BATCH = 128
DEC_SEQ = 8
PAST_LEN = 16384
PAGE_SIZE = 128

F32 = jnp.float32
N_META = 16
NORM_EPS = 1e-6
NEG = -1e30
HEAD_A = 64
D_A = D_MODEL // 2
H_A = D_A // HEAD_A
DECAY_LORA = 96
AAA_LORA = 96
GATE_LORA = 256
RWKV_COLS = 3 * D_A + DECAY_LORA + AAA_LORA + GATE_LORA
LNX_EPS = 64e-5
QK_NOPE = 64
QK_ROPE = 32
V_HEAD = 64
H_B = (D_MODEL // 2) // V_HEAD
Q_LORA = 512
KV_LORA = 256
ROPE_THETA = 10000.0
Q_BLOCK = 128
OFF_Q = 0
OFF_KV = OFF_Q + Q_LORA
OFF_KR = OFF_KV + KV_LORA
OFF_RWKV = OFF_KR + QK_ROPE
OFF_GA = OFF_RWKV + RWKV_COLS
OFF_GB = OFF_GA + D_MODEL
IN_COLS = OFF_GB + D_MODEL
N_GROUPS = 4
EXPERTS_PER_GROUP = 8
N_EXPERTS = N_GROUPS * EXPERTS_PER_GROUP
TOP_K = 2
EXPERT_FF = 512
MOE_BLOCK = 1024

kernel_name = 'rwkv7_mla_gated_hier_moe_step'


def rmsnorm(x, g):
    xf = x.astype(F32)
    y = xf * lax.rsqrt(jnp.mean(xf * xf, -1, keepdims=True) + NORM_EPS)
    return (y * g.astype(F32)).astype(x.dtype)


def rope(x, pos):
    half = x.shape[-1] // 2
    inv = ROPE_THETA ** (-jnp.arange(half, dtype=F32) / half)
    ang = pos.astype(F32)[:, None] * inv[None, :]
    ang = ang.reshape(ang.shape[:1] + (1,) * (x.ndim - 3) + ang.shape[1:])
    cos, sin = jnp.cos(ang), jnp.sin(ang)
    xf = x.astype(F32)
    x1, x2 = xf[..., :half], xf[..., half:]
    return jnp.concatenate([x1 * cos - x2 * sin, x1 * sin + x2 * cos], -1).astype(x.dtype)


def rwkv_prepare(p_rw, prev_row, lp):
    b, t = p_rw.shape[:2]
    prev = jnp.concatenate([prev_row[:, None].astype(p_rw.dtype), p_rw[:, :-1]], axis=1)
    xs = p_rw + (prev - p_rw) * lp['rwkv_mu']
    cuts = np.cumsum([D_A, D_A, D_A, DECAY_LORA, AAA_LORA]).tolist()
    r, k, v, wl, al, gl = jnp.split(xs, cuts, axis=-1)
    w_log = -jax.nn.softplus(-(lp['rwkv_w0'] + jnp.tanh(wl) @ lp['rwkv_w2']).astype(F32)) - 0.5
    decay = jnp.exp(-jnp.exp(w_log))
    a = jax.nn.sigmoid((lp['rwkv_a0'] + al @ lp['rwkv_a2']).astype(F32))
    g = jax.nn.sigmoid(gl) @ lp['rwkv_g2']
    heads = lambda z: z.reshape(b, t, H_A, HEAD_A)
    kk = heads((k * lp['rwkv_k_k']).astype(F32))
    kk = kk * lax.rsqrt(jnp.maximum(jnp.sum(kk * kk, -1, keepdims=True), 1e-24))
    k_mod = k.astype(F32) * (1.0 + (a - 1.0) * lp['rwkv_k_a'].astype(F32))
    return heads(r.astype(F32)), heads(decay), heads(k_mod), heads(v.astype(F32)), kk, heads(a), g


def rwkv_scan(r, decay, k, v, kk, a, s0):
    def step(state, inp):
        r_t, w_t, k_t, v_t, kk_t, a_t = inp
        sa = jnp.einsum('bhvk,bhk->bhv', state, -kk_t)
        state = (state * w_t[:, :, None, :] + sa[..., None] * (kk_t * a_t)[:, :, None, :]
                 + v_t[..., None] * k_t[:, :, None, :])
        return state, jnp.einsum('bhvk,bhk->bhv', state, r_t)
    xs = tuple(jnp.moveaxis(z, 1, 0) for z in (r, decay, k, v, kk, a))
    s_last, y = lax.scan(step, s0, xs)
    return jnp.moveaxis(y, 0, 1), s_last


def rwkv_output(y, r, k, v, g, lp):
    b, t = y.shape[:2]
    mu = jnp.mean(y, -1, keepdims=True)
    var = jnp.mean(jnp.square(y - mu), -1, keepdims=True)
    yn = ((y - mu) * lax.rsqrt(var + LNX_EPS)).reshape(b, t, D_A)
    yn = yn * lp['rwkv_ln_g'].astype(F32) + lp['rwkv_ln_b'].astype(F32)
    bonus = (jnp.sum(r * k * lp['rwkv_r_k'].astype(F32), -1, keepdims=True) * v).reshape(b, t, D_A)
    return ((yn + bonus) * g.astype(F32)).astype(g.dtype)


def mla_project(p, pos, lp):
    b, t = p.shape[:2]
    c_q = rmsnorm(p[..., OFF_Q:OFF_KV], lp['mla_q_norm_g'])
    q = (c_q @ lp['mla_w_uq']).reshape(b, t, H_B, QK_NOPE + QK_ROPE)
    q_nope, q_rope = q[..., :QK_NOPE], rope(q[..., QK_NOPE:], pos)
    ckv = rmsnorm(p[..., OFF_KV:OFF_KR], lp['mla_kv_norm_g'])
    krope = rope(p[..., OFF_KR:OFF_RWKV], pos)
    return q_nope, q_rope, ckv, krope


def mla_attend_prompt(q_nope, q_rope, ckv, krope, lp):
    b, t = ckv.shape[:2]
    k_nope = jnp.einsum('btc,chn->bthn', ckv, lp['mla_w_uk'])
    v = jnp.einsum('btc,chv->bthv', ckv, lp['mla_w_uv'])
    k = jnp.concatenate([k_nope, jnp.broadcast_to(krope[:, :, None], (b, t, H_B, QK_ROPE))], -1)
    q = jnp.concatenate([q_nope, q_rope], -1)
    kpos = jnp.arange(t)
    scale = (QK_NOPE + QK_ROPE) ** -0.5

    def block(args):
        qb, qpos = args
        sc = jnp.einsum('bqhd,bkhd->bhqk', qb, k).astype(F32) * scale
        sc = jnp.where(kpos[None, :] <= qpos[:, None], sc, NEG)
        pr = jax.nn.softmax(sc, axis=-1).astype(v.dtype)
        return jnp.einsum('bhqk,bkhv->bqhv', pr, v)

    o_meta = block((q[:, :N_META], jnp.arange(N_META)))
    n_real = t - N_META
    n_blk = n_real // Q_BLOCK
    qb = q[:, N_META:].reshape(b, n_blk, Q_BLOCK, H_B, QK_NOPE + QK_ROPE).swapaxes(0, 1)
    pb = (N_META + jnp.arange(n_real)).reshape(n_blk, Q_BLOCK)
    o_real = lax.map(block, (qb, pb)).swapaxes(0, 1).reshape(b, n_real, H_B, V_HEAD)
    return jnp.concatenate([o_meta, o_real], axis=1).reshape(b, t, H_B * V_HEAD)


def mla_attend_sample(q_nope, q_rope, ckv, krope, cache_ckv, cache_krope, layer, page_table, lp):
    db, ds = ckv.shape[:2]
    scale = (QK_NOPE + QK_ROPE) ** -0.5
    q_lat = jnp.einsum('bshn,chn->bshc', q_nope, lp['mla_w_uk']).astype(F32)
    q_r = q_rope.astype(F32)

    def scores(kc, kr):
        return (jnp.einsum('bshc,bkc->bhsk', q_lat, kc) + jnp.einsum('bshr,bkr->bhsk', q_r, kr)) * scale

    def update(carry, sc, kc):
        m, l, acc = carry
        m_new = jnp.maximum(m, sc.max(-1))
        corr = jnp.exp(m - m_new)
        pr = jnp.exp(sc - m_new[..., None])
        return (m_new, l * corr + pr.sum(-1),
                acc * corr[..., None] + jnp.einsum('bhsk,bkc->bhsc', pr, kc))

    def page_step(carry, phys):
        kc = cache_ckv[layer, phys].astype(F32)
        kr = cache_krope[layer, phys].astype(F32)
        return update(carry, scores(kc, kr), kc), None

    init = (jnp.full((db, H_B, ds), NEG, F32), jnp.zeros((db, H_B, ds), F32),
            jnp.zeros((db, H_B, ds, KV_LORA), F32))
    carry, _ = lax.scan(page_step, init, page_table.T)
    kc_new, kr_new = ckv.astype(F32), krope.astype(F32)
    causal = jnp.arange(ds)[None, :] <= jnp.arange(ds)[:, None]
    sc_new = jnp.where(causal, scores(kc_new, kr_new), NEG)
    _, l, acc = update(carry, sc_new, kc_new)
    o_lat = (acc / l[..., None]).astype(ckv.dtype)
    o = jnp.einsum('bhsc,chv->bshv', o_lat, lp['mla_w_uv'])
    return o.reshape(db, ds, H_B * V_HEAD)


def token_mixer(h, pos, shift_prev, wkv_prev, attend, lp):
    p = h @ lp['w_in']
    p_rw = p[..., OFF_RWKV:OFF_GA]
    r, decay, k, v, kk, a, g = rwkv_prepare(p_rw, shift_prev, lp)
    y_state, wkv_new = rwkv_scan(r, decay, k, v, kk, a, wkv_prev.astype(F32))
    o_a = rwkv_output(y_state, r, k, v, g, lp)
    q_nope, q_rope, ckv, krope = mla_project(p, pos, lp)
    o_b = attend(q_nope, q_rope, ckv, krope)
    gate_a = jax.nn.sigmoid(p[..., OFF_GA:OFF_GB])
    gate_b = jax.nn.sigmoid(p[..., OFF_GB:])
    merged = gate_a * (o_a @ lp['w_up_a']) + gate_b * (o_b @ lp['w_up_b'])
    return merged @ lp['w_o'], (ckv, krope, wkv_new, p_rw[:, -1])


def moe_ffn(h, lp):
    shp = h.shape
    x = h.reshape(-1, D_MODEL)
    n = x.shape[0]
    xf = x.astype(F32)
    g_logit = xf @ lp['router_group_w'].astype(F32)
    g_prob = jax.nn.softmax(g_logit, axis=-1)
    g_sel = jnp.argmax(g_logit + lp['router_group_b'].astype(F32), axis=-1)
    e_logit = (xf @ lp['router_expert_w'].astype(F32)).reshape(n, N_GROUPS, EXPERTS_PER_GROUP)
    e_in = jnp.take_along_axis(e_logit, g_sel[:, None, None], axis=1)[:, 0]
    e_bias = lp['router_expert_b'].astype(F32).reshape(N_GROUPS, EXPERTS_PER_GROUP)[g_sel]
    _, idx = lax.top_k(e_in + e_bias, TOP_K)
    gate = (jax.nn.softmax(jnp.take_along_axis(e_in, idx, axis=1), axis=-1)
            * jnp.take_along_axis(g_prob, g_sel[:, None], axis=1))
    eid = g_sel[:, None] * EXPERTS_PER_GROUP + idx
    comb = jnp.sum(jax.nn.one_hot(eid, N_EXPERTS, dtype=F32) * gate[..., None], axis=1)
    n_blk = -(-n // MOE_BLOCK)
    pad = n_blk * MOE_BLOCK - n
    xb = jnp.pad(x, ((0, pad), (0, 0))).reshape(n_blk, MOE_BLOCK, D_MODEL)
    cb = jnp.pad(comb, ((0, pad), (0, 0))).reshape(n_blk, MOE_BLOCK, N_EXPERTS).astype(x.dtype)

    def block(args):
        xt, ct = args
        hg = jnp.einsum('td,edf->tef', xt, lp['expert_w_gate'])
        hu = jnp.einsum('td,edf->tef', xt, lp['expert_w_up'])
        return jnp.einsum('tef,efd->td', jax.nn.silu(hg) * hu * ct[..., None], lp['expert_w_down'])

    y = lax.map(block, (xb, cb)).reshape(n_blk * MOE_BLOCK, D_MODEL)[:n]
    return y.reshape(shp)


def stack_state(states, i):
    return jnp.stack([st[i] for st in states], axis=0)


def setup_inputs(seed: int = 0) -> dict:
    key = jax.random.key(seed)
    ks = iter(jax.random.split(key, 48))
    nrm = lambda shape, scale: jax.random.normal(next(ks), shape, F32) * scale
    uni = lambda shape, lo, hi: jax.random.uniform(next(ks), shape, F32, lo, hi)
    n_pages = PAST_LEN // PAGE_SIZE
    n_used = DEC_BATCH * n_pages
    n_pool = n_used + max(1, n_used // 4)
    page_table = jax.random.permutation(next(ks), n_pool)[:n_used].reshape(DEC_BATCH, n_pages).astype(jnp.int32)
    L, D = DEPTH, D_MODEL
    return {
        'x_prompt': nrm((BATCH, SEQ, D), 1.0),
        'x_sample': nrm((DEC_BATCH, DEC_SEQ, D), 1.0),
        'cache_ckv': nrm((L, n_pool, PAGE_SIZE, KV_LORA), 1.0),
        'cache_krope': nrm((L, n_pool, PAGE_SIZE, QK_ROPE), 1.0),
        'state_wkv': nrm((L, DEC_BATCH, H_A, HEAD_A, HEAD_A), 0.3),
        'state_shift': nrm((L, DEC_BATCH, RWKV_COLS), 1.0),
        'page_table': page_table,
        'meta_tokens': nrm((N_META, D), 1.0),
        'norm_mix_g': 1.0 + nrm((L, D), 0.02),
        'w_in': nrm((L, D, IN_COLS), D ** -0.5),
        'rwkv_mu': uni((L, RWKV_COLS), 0.0, 1.0),
        'rwkv_w0': uni((L, D_A), -6.0, -1.0),
        'rwkv_w2': nrm((L, DECAY_LORA, D_A), 0.1 * DECAY_LORA ** -0.5),
        'rwkv_a0': nrm((L, D_A), 0.1),
        'rwkv_a2': nrm((L, AAA_LORA, D_A), AAA_LORA ** -0.5),
        'rwkv_g2': nrm((L, GATE_LORA, D_A), GATE_LORA ** -0.5),
        'rwkv_k_k': 0.85 + nrm((L, D_A), 0.02),
        'rwkv_k_a': 1.0 + nrm((L, D_A), 0.02),
        'rwkv_r_k': nrm((L, H_A, HEAD_A), 0.1),
        'rwkv_ln_g': 1.0 + nrm((L, D_A), 0.02),
        'rwkv_ln_b': nrm((L, D_A), 0.02),
        'mla_q_norm_g': 1.0 + nrm((L, Q_LORA), 0.02),
        'mla_w_uq': nrm((L, Q_LORA, H_B * (QK_NOPE + QK_ROPE)), Q_LORA ** -0.5),
        'mla_kv_norm_g': 1.0 + nrm((L, KV_LORA), 0.02),
        'mla_w_uk': nrm((L, KV_LORA, H_B, QK_NOPE), KV_LORA ** -0.5),
        'mla_w_uv': nrm((L, KV_LORA, H_B, V_HEAD), KV_LORA ** -0.5),
        'w_up_a': nrm((L, D_A, D), D_A ** -0.5),
        'w_up_b': nrm((L, H_B * V_HEAD, D), (H_B * V_HEAD) ** -0.5),
        'w_o': nrm((L, D, D), D ** -0.5),
        'norm_ffn_g': 1.0 + nrm((L, D), 0.02),
        'router_group_w': nrm((L, D, N_GROUPS), D ** -0.5),
        'router_group_b': nrm((L, N_GROUPS), 0.01),
        'router_expert_w': nrm((L, D, N_EXPERTS), D ** -0.5),
        'router_expert_b': nrm((L, N_EXPERTS), 0.01),
        'expert_w_gate': nrm((L, N_EXPERTS, D, EXPERT_FF), D ** -0.5),
        'expert_w_up': nrm((L, N_EXPERTS, D, EXPERT_FF), D ** -0.5),
        'expert_w_down': nrm((L, N_EXPERTS, EXPERT_FF, D), EXPERT_FF ** -0.5),
        'norm_final_g': 1.0 + nrm((D,), 0.02),
    }


def reference(x_prompt, x_sample, cache_ckv, cache_krope, state_wkv, state_shift, page_table,
              meta_tokens, norm_mix_g, w_in, rwkv_mu, rwkv_w0, rwkv_w2, rwkv_a0, rwkv_a2,
              rwkv_g2, rwkv_k_k, rwkv_k_a, rwkv_r_k, rwkv_ln_g, rwkv_ln_b, mla_q_norm_g,
              mla_w_uq, mla_kv_norm_g, mla_w_uk, mla_w_uv, w_up_a, w_up_b, w_o, norm_ffn_g,
              router_group_w, router_group_b, router_expert_w, router_expert_b,
              expert_w_gate, expert_w_up, expert_w_down, norm_final_g):
    b, s = x_prompt.shape[:2]
    ds = x_sample.shape[1]
    past = page_table.shape[1] * PAGE_SIZE
    meta = jnp.broadcast_to(meta_tokens.astype(x_prompt.dtype)[None], (b, N_META, D_MODEL))
    xp = jnp.concatenate([meta, x_prompt], axis=1)
    xs = x_sample
    pos_p = jnp.arange(N_META + s)
    pos_s = past + jnp.arange(ds)
    new_p, new_s = [], []
    for l in range(DEPTH):
        lp = dict(w_in=w_in[l], rwkv_mu=rwkv_mu[l], rwkv_w0=rwkv_w0[l], rwkv_w2=rwkv_w2[l],
                  rwkv_a0=rwkv_a0[l], rwkv_a2=rwkv_a2[l], rwkv_g2=rwkv_g2[l],
                  rwkv_k_k=rwkv_k_k[l], rwkv_k_a=rwkv_k_a[l], rwkv_r_k=rwkv_r_k[l],
                  rwkv_ln_g=rwkv_ln_g[l], rwkv_ln_b=rwkv_ln_b[l],
                  mla_q_norm_g=mla_q_norm_g[l], mla_w_uq=mla_w_uq[l],
                  mla_kv_norm_g=mla_kv_norm_g[l], mla_w_uk=mla_w_uk[l], mla_w_uv=mla_w_uv[l],
                  w_up_a=w_up_a[l], w_up_b=w_up_b[l], w_o=w_o[l],
                  router_group_w=router_group_w[l], router_group_b=router_group_b[l],
                  router_expert_w=router_expert_w[l], router_expert_b=router_expert_b[l],
                  expert_w_gate=expert_w_gate[l], expert_w_up=expert_w_up[l],
                  expert_w_down=expert_w_down[l])
        mix_p, st_p = token_mixer(rmsnorm(xp, norm_mix_g[l]), pos_p,
                                  jnp.zeros((b, RWKV_COLS), xp.dtype),
                                  jnp.zeros((b, H_A, HEAD_A, HEAD_A), F32),
                                  functools.partial(mla_attend_prompt, lp=lp), lp)
        xp = xp + mix_p
        xp = xp + moe_ffn(rmsnorm(xp, norm_ffn_g[l]), lp)
        mix_s, st_s = token_mixer(rmsnorm(xs, norm_mix_g[l]), pos_s, state_shift[l], state_wkv[l],
                                  functools.partial(mla_attend_sample, cache_ckv=cache_ckv,
                                                    cache_krope=cache_krope, layer=l,
                                                    page_table=page_table, lp=lp), lp)
        xs = xs + mix_s
        xs = xs + moe_ffn(rmsnorm(xs, norm_ffn_g[l]), lp)
        new_p.append(st_p)
        new_s.append(st_s)
    y_prompt = rmsnorm(xp[:, N_META:], norm_final_g)
    y_sample = rmsnorm(xs, norm_final_g)
    ckv_p, kr_p, wkv_p, sh_p = (stack_state(new_p, 0), stack_state(new_p, 1),
                                stack_state(new_p, 2), stack_state(new_p, 3))
    ckv_s, kr_s, wkv_s, sh_s = (stack_state(new_s, 0), stack_state(new_s, 1),
                                stack_state(new_s, 2), stack_state(new_s, 3))
    return (y_prompt, y_sample, ckv_p, kr_p, wkv_p, sh_p, ckv_s, kr_s, wkv_s, sh_s)
```

```python
import functools

import jax
import jax.numpy as jnp
import numpy as np
from jax import lax
from jax.experimental import pallas as pl
from jax.experimental.pallas import tpu as pltpu

F32 = jnp.float32
BF16 = jnp.bfloat16
HIGHEST = lax.Precision.HIGHEST

N_META = 16
NORM_EPS = 1e-6
NEG = -1e30
HEAD_A = 64
DECAY_LORA = 96
AAA_LORA = 96
GATE_LORA = 256
LNX_EPS = 64e-5
QK_NOPE = 64
QK_ROPE = 32
V_HEAD = 64
Q_LORA = 512
KV_LORA = 256
ROPE_THETA = 10000.0
PAGE_SIZE = 128
N_GROUPS = 4
EXPERTS_PER_GROUP = 8
N_EXPERTS = N_GROUPS * EXPERTS_PER_GROUP

LANES = 128
HEAD_PAIR = 2 * HEAD_A
SEG = 512
LATENT_PAD = KV_LORA + LANES
VMEM_LIMIT = 56 * 1024 * 1024
SCAN_CHUNK = 16
EXPERT_TILE = 256
PAGES_PER_STEP = 16


def _pick(n, target, mult=8):
    best = None
    for d in range(mult, min(n, target) + 1, mult):
        if n % d == 0:
            best = d
    return best if best is not None else n


def _params(*sem):
    return pltpu.CompilerParams(dimension_semantics=sem, vmem_limit_bytes=VMEM_LIMIT)


def _dot(a, b, **kw):
    return jnp.dot(a, b, preferred_element_type=F32, **kw)


def _dot_nt(a, b):
    return lax.dot_general(a, b, (((1,), (1,)), ((), ())), preferred_element_type=F32)


def _dot_tn(a, b):
    return lax.dot_general(a, b, (((0,), (0,)), ((), ())), preferred_element_type=F32)


def _sigmoid(x):
    return 1.0 / (1.0 + jnp.exp(-x))


def _head_ones(scale=1.0):
    r = lax.broadcasted_iota(jnp.int32, (LANES, LANES), 0) // HEAD_A
    c = lax.broadcasted_iota(jnp.int32, (LANES, LANES), 1) // HEAD_A
    return jnp.where(r == c, scale, 0.0).astype(F32)


def _rms_mm_kernel(x_ref, g_ref, w_ref, o_ref, h_ref):
    @pl.when(pl.program_id(1) == 0)
    def _():
        x = x_ref[...]
        ms = jnp.mean(x * x, axis=-1, keepdims=True)
        h_ref[...] = (x * lax.rsqrt(ms + NORM_EPS) * g_ref[...]).astype(BF16)

    o_ref[...] = _dot(h_ref[...], w_ref[...])


def rms_matmul(x, g, w):
    n, k = x.shape
    m = w.shape[1]
    tm = _pick(n, 1376)
    tn = SEG
    return pl.pallas_call(
        _rms_mm_kernel,
        grid=(n // tm, m // tn),
        in_specs=[pl.BlockSpec((tm, k), lambda i, j: (i, 0)),
                  pl.BlockSpec((1, k), lambda i, j: (0, 0)),
                  pl.BlockSpec((k, tn), lambda i, j: (0, j))],
        out_specs=pl.BlockSpec((tm, tn), lambda i, j: (i, j)),
        out_shape=jax.ShapeDtypeStruct((n, m), F32),
        scratch_shapes=[pltpu.VMEM((tm, k), BF16)],
        compiler_params=_params("parallel", "arbitrary"),
        name="rms_matmul",
    )(x, g, w)


def _prepare_kernel(pr_ref, pk_ref, pv_ref, pl_ref, sh_ref, mu_ref, w0_ref, a0_ref, kk_ref, ka_ref,
                    w2_ref, a2_ref, g2_ref,
                    r_out, lw_out, k_out, v_out, kk_out, b_out, g_out, carry_ref):
    d_a = pr_ref.shape[-1]

    @pl.when(pl.program_id(1) == 0)
    def _():
        carry_ref[...] = sh_ref[...]

    def shifted(ref, lo, width):
        x = ref[...]
        prev_row = carry_ref[:, :, lo:lo + width]
        t_idx = lax.broadcasted_iota(jnp.int32, x.shape, 1)
        prev = jnp.where(t_idx == 0, prev_row, pltpu.roll(x, 1, axis=1))
        carry_ref[:, :, lo:lo + width] = x[:, x.shape[1] - 1:, :]
        xs = x + (prev - x) * mu_ref[:, lo:lo + width]
        return xs.reshape(x.shape[0] * x.shape[1], width)

    shape3 = r_out.shape
    r = shifted(pr_ref, 0, d_a)
    k = shifted(pk_ref, d_a, d_a)
    v = shifted(pv_ref, 2 * d_a, d_a)
    lora = shifted(pl_ref, 3 * d_a, SEG)
    wl, al, gl = lora[:, :LANES], lora[:, LANES:2 * LANES], lora[:, 2 * LANES:]

    z = -(w0_ref[...] + _dot(jnp.tanh(wl).astype(BF16), w2_ref[...]))
    softplus = jnp.maximum(z, 0.0) + jnp.log(1.0 + jnp.exp(-jnp.abs(z)))
    lw = -jnp.exp(-softplus - 0.5)
    a = _sigmoid(a0_ref[...] + _dot(al.astype(BF16), a2_ref[...]))
    g = _dot(_sigmoid(gl).astype(BF16), g2_ref[...])

    kkf = k * kk_ref[...]
    ones = _head_ones()
    ssq = jnp.concatenate(
        [_dot(jnp.square(kkf[:, j * LANES:(j + 1) * LANES]), ones, precision=HIGHEST)
         for j in range(d_a // LANES)], axis=-1)
    kk = kkf * lax.rsqrt(jnp.maximum(ssq, 1e-24))
    k_mod = k * (1.0 + (a - 1.0) * ka_ref[...])

    r_out[...] = r.reshape(shape3)
    lw_out[...] = lw.reshape(shape3)
    k_out[...] = k_mod.reshape(shape3)
    v_out[...] = v.reshape(shape3)
    kk_out[...] = kk.reshape(shape3)
    b_out[...] = (kk * a).reshape(shape3)
    g_out[...] = g.reshape(shape3)


def rwkv_prepare(p3, shift3, mu, w0, a0, k_k, k_a, w2, a2, g2, bb, tt):
    b, t, _ = p3.shape
    d_a = w0.shape[-1]
    nblk = d_a // SEG
    row = lambda c: pl.BlockSpec((bb, tt, d_a), lambda i, j, c=c: (i, j, c))
    vec = lambda n: pl.BlockSpec((1, n), lambda i, j: (0, 0))
    mat = lambda a: pl.BlockSpec(a.shape, lambda i, j: (0, 0))
    out = jax.ShapeDtypeStruct((b, t, d_a), F32)
    return pl.pallas_call(
        _prepare_kernel,
        grid=(b // bb, t // tt),
        in_specs=[row(0), row(1), row(2),
                  pl.BlockSpec((bb, tt, SEG), lambda i, j: (i, j, 3 * nblk)),
                  pl.BlockSpec((bb, 1, 3 * d_a + SEG), lambda i, j: (i, 0, 0)),
                  vec(3 * d_a + SEG), vec(d_a), vec(d_a), vec(d_a), vec(d_a),
                  mat(w2), mat(a2), mat(g2)],
        out_specs=[pl.BlockSpec((bb, tt, d_a), lambda i, j: (i, j, 0))] * 7,
        out_shape=[out] * 7,
        scratch_shapes=[pltpu.VMEM((bb, 1, 3 * d_a + SEG), F32)],
        compiler_params=_params("parallel", "arbitrary"),
        name="rwkv_prepare",
    )(p3, p3, p3, p3, shift3, mu, w0, a0, k_k, k_a, w2, a2, g2)


def _scan_kernel(r_ref, lw_ref, k_ref, v_ref, kk_ref, b_ref, s0_ref, y_ref, sf_ref, st_ref, *, chunk):
    c_idx = pl.program_id(1)
    n_pairs = st_ref.shape[0]

    @pl.when(c_idx == 0)
    def _():
        st_ref[...] = s0_ref[0]

    ri = lax.broadcasted_iota(jnp.int32, (chunk, chunk), 0)
    ci = lax.broadcasted_iota(jnp.int32, (chunk, chunk), 1)
    tri_incl = jnp.where(ci <= ri, 1.0, 0.0).astype(F32)
    r2 = lax.broadcasted_iota(jnp.int32, (2 * chunk, 2 * chunk), 0)
    c2 = lax.broadcasted_iota(jnp.int32, (2 * chunk, 2 * chunk), 1)
    strict = c2 < r2
    incl = c2 <= r2
    head0 = lax.broadcasted_iota(jnp.int32, (chunk, LANES), 1) < HEAD_A

    def stack2(x):
        return jnp.concatenate([jnp.where(head0, x, 0.0), jnp.where(head0, 0.0, x)], axis=0).astype(BF16)

    for j in range(n_pairs):
        sl = slice(j * LANES, (j + 1) * LANES)
        r, lw, k, v = r_ref[0, :, sl], lw_ref[0, :, sl], k_ref[0, :, sl], v_ref[0, :, sl]
        kk, b = kk_ref[0, :, sl], b_ref[0, :, sl]
        s_prev = st_ref[j]
        s_bf = s_prev.astype(BF16)

        cum = _dot(tri_incl, lw, precision=HIGHEST)
        cum_last = cum[chunk - 1:chunk, :]
        e_neg = jnp.exp(-cum)
        e_tail = jnp.exp(cum_last - cum)
        r_t = r * jnp.exp(cum)
        q_u = stack2(kk * jnp.exp(cum - lw))
        q_r = stack2(r_t)
        k_h = stack2(k * e_neg)
        b_h = stack2(b * e_neg)
        v2 = stack2(v)

        neg_l = jnp.where(strict, -_dot_nt(q_u, b_h), 0.0)
        a_vk = jnp.where(strict, _dot_nt(q_u, k_h), 0.0)
        x = _dot_nt(q_u, s_bf) + _dot(a_vk.astype(BF16), v2)
        power = neg_l
        steps = max(1, int(np.ceil(np.log2(chunk))))
        for s in range(steps):
            pb = power.astype(BF16)
            x = x + _dot(pb, x.astype(BF16))
            if s + 1 < steps:
                power = _dot(pb, pb)
        u2 = x.astype(BF16)

        a_rk = jnp.where(incl, _dot_nt(q_r, k_h), 0.0)
        a_rb = jnp.where(incl, _dot_nt(q_r, b_h), 0.0)
        y2 = _dot(a_rk.astype(BF16), v2) - _dot(a_rb.astype(BF16), u2)
        y_ref[0, :, sl] = _dot_nt(r_t.astype(BF16), s_bf) + y2[:chunk] + y2[chunk:]

        lhs = jnp.concatenate([v2, -u2], axis=0)
        rhs = jnp.concatenate([stack2(k * e_tail), stack2(b * e_tail)], axis=0)
        st_ref[j] = s_prev * jnp.exp(cum_last) + _dot_tn(lhs, rhs)

    @pl.when(c_idx == pl.num_programs(1) - 1)
    def _():
        sf_ref[0] = st_ref[...]


def rwkv_scan(r, lw, k, v, kk, b, s0, chunk):
    bsz, t, d_a = r.shape
    n_pairs = d_a // LANES
    seq = pl.BlockSpec((1, chunk, d_a), lambda i, c: (i, c, 0))
    st = pl.BlockSpec((1, n_pairs, LANES, LANES), lambda i, c: (i, 0, 0, 0))
    return pl.pallas_call(
        functools.partial(_scan_kernel, chunk=chunk),
        grid=(bsz, t // chunk),
        in_specs=[seq] * 6 + [st],
        out_specs=[seq, st],
        out_shape=[jax.ShapeDtypeStruct((bsz, t, d_a), F32),
                   jax.ShapeDtypeStruct((bsz, n_pairs, LANES, LANES), F32)],
        scratch_shapes=[pltpu.VMEM((n_pairs, LANES, LANES), F32)],
        compiler_params=_params("parallel", "arbitrary"),
        name="rwkv_scan",
    )(r, lw, k, v, kk, b, s0)


def _rwkv_out_kernel(y_ref, r_ref, k_ref, v_ref, g_ref, lng_ref, lnb_ref, rk_ref, o_ref):
    mean_m = _head_ones(1.0 / HEAD_A)
    sum_m = _head_ones()
    for j in range(y_ref.shape[-1] // LANES):
        sl = slice(j * LANES, (j + 1) * LANES)
        y = y_ref[:, sl]
        d = y - _dot(y, mean_m, precision=HIGHEST)
        var = _dot(d * d, mean_m, precision=HIGHEST)
        yn = d * lax.rsqrt(var + LNX_EPS) * lng_ref[:, sl] + lnb_ref[:, sl]
        bonus = _dot(r_ref[:, sl] * k_ref[:, sl] * rk_ref[:, sl], sum_m, precision=HIGHEST) * v_ref[:, sl]
        o_ref[:, sl] = ((yn + bonus) * g_ref[:, sl]).astype(o_ref.dtype)


def rwkv_out(y, r, k, v, g, ln_g, ln_b, r_k):
    n, d_a = y.shape
    tm = _pick(n, 688)
    row = pl.BlockSpec((tm, d_a), lambda i: (i, 0))
    vec = pl.BlockSpec((1, d_a), lambda i: (0, 0))
    return pl.pallas_call(
        _rwkv_out_kernel,
        grid=(n // tm,),
        in_specs=[row] * 5 + [vec] * 3,
        out_specs=row,
        out_shape=jax.ShapeDtypeStruct((n, d_a), BF16),
        compiler_params=_params("parallel"),
        name="rwkv_out",
    )(y, r, k, v, g, ln_g, ln_b, r_k)


def _mla_kernel(pq_ref, pkv_ref, cos_ref, sin_ref, gq_ref, gkv_ref, wn_ref, wr_ref, wrs_ref, wuk_ref,
                qf_ref, kf_ref, ckv_ref, kr_ref, *, scale):
    bb, tt, _ = pq_ref.shape
    rows = bb * tt
    n_heads = qf_ref.shape[1]
    cos = jnp.broadcast_to(cos_ref[...][None], (bb, tt, LANES)).reshape(rows, LANES)
    sin = jnp.broadcast_to(sin_ref[...][None], (bb, tt, LANES)).reshape(rows, LANES)

    pq = pq_ref[...].reshape(rows, Q_LORA)
    c_q = (pq * lax.rsqrt(jnp.mean(pq * pq, -1, keepdims=True) + NORM_EPS) * gq_ref[...]).astype(BF16)
    q_nope = _dot(c_q, wn_ref[...]).astype(BF16)
    q_rope = _dot(c_q, wr_ref[...])
    q_rope_sw = _dot(c_q, wrs_ref[...])
    for j in range(n_heads // 2):
        lat = _dot(q_nope[:, j * LANES:(j + 1) * LANES], wuk_ref[j]) * scale
        for hh in range(2):
            h = 2 * j + hh
            qf_ref[:, h, :, :KV_LORA] = lat[:, hh * KV_LORA:(hh + 1) * KV_LORA].reshape(bb, tt, KV_LORA).astype(BF16)
            sl = slice(h * LANES, (h + 1) * LANES)
            roped = (q_rope[:, sl] * cos + q_rope_sw[:, sl] * sin) * scale
            qf_ref[:, h, :, KV_LORA:] = roped.reshape(bb, tt, LANES).astype(BF16)

    pkv = pkv_ref[...].reshape(rows, SEG)
    kv = pkv[:, :KV_LORA]
    ckv = kv * lax.rsqrt(jnp.mean(kv * kv, -1, keepdims=True) + NORM_EPS) * gkv_ref[...]
    krope = pkv[:, KV_LORA:KV_LORA + LANES] * cos + pkv[:, KV_LORA + LANES:] * sin
    ckv_ref[...] = ckv.reshape(bb, tt, KV_LORA)
    kr_ref[...] = krope[:, :QK_ROPE].reshape(bb, tt, QK_ROPE)
    kf_ref[:, :, :KV_LORA] = ckv.reshape(bb, tt, KV_LORA).astype(BF16)
    kf_ref[:, :, KV_LORA:] = krope.reshape(bb, tt, LANES).astype(BF16)


def mla_project(p3, cos_t, sin_t, g_q, g_kv, w_nope, w_rope, w_rope_sw, w_uk_pairs, n_heads, bb, tt):
    b, t, width = p3.shape
    q_blk = (width - 2 * SEG) // SEG
    full = lambda a: pl.BlockSpec(a.shape, lambda i, j: (0,) * a.ndim)
    scale = float((QK_NOPE + QK_ROPE) ** -0.5)
    return pl.pallas_call(
        functools.partial(_mla_kernel, scale=scale),
        grid=(b // bb, t // tt),
        in_specs=[pl.BlockSpec((bb, tt, SEG), lambda i, j: (i, j, q_blk)),
                  pl.BlockSpec((bb, tt, SEG), lambda i, j: (i, j, q_blk + 1)),
                  pl.BlockSpec((tt, LANES), lambda i, j: (j, 0)),
                  pl.BlockSpec((tt, LANES), lambda i, j: (j, 0)),
                  full(g_q), full(g_kv), full(w_nope), full(w_rope), full(w_rope_sw), full(w_uk_pairs)],
        out_specs=[pl.BlockSpec((bb, n_heads, tt, LATENT_PAD), lambda i, j: (i, 0, j, 0)),
                   pl.BlockSpec((bb, tt, LATENT_PAD), lambda i, j: (i, j, 0)),
                   pl.BlockSpec((bb, tt, KV_LORA), lambda i, j: (i, j, 0)),
                   pl.BlockSpec((bb, tt, QK_ROPE), lambda i, j: (i, j, 0))],
        out_shape=[jax.ShapeDtypeStruct((b, n_heads, t, LATENT_PAD), BF16),
                   jax.ShapeDtypeStruct((b, t, LATENT_PAD), BF16),
                   jax.ShapeDtypeStruct((b, t, KV_LORA), F32),
                   jax.ShapeDtypeStruct((b, t, QK_ROPE), F32)],
        compiler_params=_params("parallel", "parallel"),
        name="mla_project",
    )(p3, p3, cos_t, sin_t, g_q, g_kv, w_nope, w_rope, w_rope_sw, w_uk_pairs)


def _value_up(o_lat, wuv_ref, store):
    n_heads = o_lat.shape[0]
    for j in range(n_heads // 2):
        lhs = jnp.concatenate([o_lat[2 * j], o_lat[2 * j + 1]], axis=-1).astype(BF16)
        store(j, _dot(lhs, wuv_ref[j]))


def _attn_prompt_kernel(q_ref, k_ref, wuv_ref, o_ref, m_ref, l_ref, acc_ref, *, tq):
    qi, ki = pl.program_id(1), pl.program_id(2)
    n_heads = q_ref.shape[1]
    rows = n_heads * tq

    @pl.when(ki == 0)
    def _():
        m_ref[...] = jnp.full(m_ref.shape, NEG, F32)
        l_ref[...] = jnp.zeros(l_ref.shape, F32)
        acc_ref[...] = jnp.zeros(acc_ref.shape, F32)

    @pl.when(ki <= qi)
    def _():
        q = q_ref[0].reshape(rows, LATENT_PAD)
        k = k_ref[0]
        s = _dot_nt(q, k).reshape(n_heads, tq, tq)
        qpos = qi * tq + lax.broadcasted_iota(jnp.int32, (tq, tq), 0)
        kpos = ki * tq + lax.broadcasted_iota(jnp.int32, (tq, tq), 1)
        s = jnp.where((kpos <= qpos)[None], s, NEG).reshape(rows, tq)
        m_prev = m_ref[...]
        m_new = jnp.maximum(m_prev, jnp.max(s, axis=-1, keepdims=True))
        corr = jnp.exp(m_prev - m_new)
        p = jnp.exp(s - m_new)
        l_ref[...] = l_ref[...] * corr + jnp.sum(p, axis=-1, keepdims=True)
        acc_ref[...] = acc_ref[...] * corr + _dot(p.astype(BF16), k[:, :KV_LORA])
        m_ref[...] = m_new

    @pl.when(ki == qi)
    def _():
        o_lat = (acc_ref[...] / l_ref[...]).reshape(n_heads, tq, KV_LORA)

        def store(j, val):
            o_ref[0, :, j * LANES:(j + 1) * LANES] = val.astype(o_ref.dtype)
        _value_up(o_lat, wuv_ref, store)


def attention_prompt(qf, kf, w_uv_pairs):
    b, n_heads, t, _ = qf.shape
    tq = _pick(t, 344)
    nq = t // tq
    return pl.pallas_call(
        functools.partial(_attn_prompt_kernel, tq=tq),
        grid=(b, nq, nq),
        in_specs=[pl.BlockSpec((1, n_heads, tq, LATENT_PAD), lambda i, q, k: (i, 0, q, 0)),
                  pl.BlockSpec((1, tq, LATENT_PAD), lambda i, q, k: (i, jnp.minimum(k, q), 0)),
                  pl.BlockSpec(w_uv_pairs.shape, lambda i, q, k: (0, 0, 0))],
        out_specs=pl.BlockSpec((1, tq, n_heads * V_HEAD), lambda i, q, k: (i, q, 0)),
        out_shape=jax.ShapeDtypeStruct((b, t, n_heads * V_HEAD), BF16),
        scratch_shapes=[pltpu.VMEM((n_heads * tq, 1), F32), pltpu.VMEM((n_heads * tq, 1), F32),
                        pltpu.VMEM((n_heads * tq, KV_LORA), F32)],
        compiler_params=_params("parallel", "parallel", "arbitrary"),
        name="attention_prompt",
    )(qf, kf, w_uv_pairs)


def _attn_sample_kernel(pt_ref, q_ref, kn_ref, wuv_ref, *rest, n_pages_step):
    ckv_refs = rest[:n_pages_step]
    kr_refs = rest[n_pages_step:2 * n_pages_step]
    o_ref, m_ref, l_ref, acc_ref = rest[2 * n_pages_step:]
    del pt_ref
    g = pl.program_id(1)
    n_heads, ds = q_ref.shape[1], q_ref.shape[2]
    rows = n_heads * ds

    @pl.when(g == 0)
    def _():
        m_ref[...] = jnp.full(m_ref.shape, NEG, F32)
        l_ref[...] = jnp.zeros(l_ref.shape, F32)
        acc_ref[...] = jnp.zeros(acc_ref.shape, F32)

    q = q_ref[0].reshape(rows, LATENT_PAD)
    q_lat, q_rope = q[:, :KV_LORA], q[:, KV_LORA:KV_LORA + QK_ROPE]

    def update(scores, values):
        m_prev = m_ref[...]
        m_new = m_prev
        for s in scores:
            m_new = jnp.maximum(m_new, jnp.max(s, axis=-1, keepdims=True))
        corr = jnp.exp(m_prev - m_new)
        l_new = l_ref[...] * corr
        acc = acc_ref[...] * corr
        for s, val in zip(scores, values):
            p = jnp.exp(s - m_new)
            l_new = l_new + jnp.sum(p, axis=-1, keepdims=True)
            acc = acc + _dot(p.astype(BF16), val)
        m_ref[...] = m_new
        l_ref[...] = l_new
        acc_ref[...] = acc

    keys = [c[...].astype(BF16) for c in ckv_refs]
    scores = [_dot_nt(q_lat, kc) + _dot_nt(q_rope, kr[...].astype(BF16)) for kc, kr in zip(keys, kr_refs)]
    update(scores, keys)

    @pl.when(g == pl.num_programs(1) - 1)
    def _():
        kn = kn_ref[0]
        s_new = _dot_nt(q, kn).reshape(n_heads, ds, ds)
        causal = (lax.broadcasted_iota(jnp.int32, (ds, ds), 1) <= lax.broadcasted_iota(jnp.int32, (ds, ds), 0))
        s_new = jnp.where(causal[None], s_new, NEG).reshape(rows, ds)
        update([s_new], [kn[:, :KV_LORA]])
        o_lat = (acc_ref[...] / l_ref[...]).reshape(n_heads, ds, KV_LORA)

        def store(j, val):
            o_ref[0, :, j * LANES:(j + 1) * LANES] = val.astype(o_ref.dtype)
        _value_up(o_lat, wuv_ref, store)


def attention_sample(qf, kf_new, cache_ckv, cache_krope, layer, page_table, w_uv_pairs):
    db, n_heads, ds, _ = qf.shape
    n_pages = page_table.shape[1]
    gp = _pick(n_pages, PAGES_PER_STEP, 1)
    page = cache_ckv.shape[2]

    def cache_spec(width, i):
        return pl.BlockSpec((None, None, page, width), lambda b, g, pt, i=i: (layer, pt[b, g * gp + i], 0, 0))

    grid_spec = pltpu.PrefetchScalarGridSpec(
        num_scalar_prefetch=1,
        grid=(db, n_pages // gp),
        in_specs=[pl.BlockSpec((1, n_heads, ds, LATENT_PAD), lambda b, g, pt: (b, 0, 0, 0)),
                  pl.BlockSpec((1, ds, LATENT_PAD), lambda b, g, pt: (b, 0, 0)),
                  pl.BlockSpec(w_uv_pairs.shape, lambda b, g, pt: (0, 0, 0))]
                 + [cache_spec(KV_LORA, i) for i in range(gp)]
                 + [cache_spec(QK_ROPE, i) for i in range(gp)],
        out_specs=pl.BlockSpec((1, ds, n_heads * V_HEAD), lambda b, g, pt: (b, 0, 0)),
        scratch_shapes=[pltpu.VMEM((n_heads * ds, 1), F32), pltpu.VMEM((n_heads * ds, 1), F32),
                        pltpu.VMEM((n_heads * ds, KV_LORA), F32)],
    )
    return pl.pallas_call(
        functools.partial(_attn_sample_kernel, n_pages_step=gp),
        grid_spec=grid_spec,
        out_shape=jax.ShapeDtypeStruct((db, ds, n_heads * V_HEAD), BF16),
        compiler_params=_params("parallel", "arbitrary"),
        name="attention_sample",
    )(page_table, qf, kf_new, w_uv_pairs, *([cache_ckv] * gp), *([cache_krope] * gp))


def _merge_kernel(oa_ref, ob_ref, ga_ref, gb_ref, wa_ref, wb_ref, o_ref):
    ua = _dot(oa_ref[...], wa_ref[...])
    ub = _dot(ob_ref[...], wb_ref[...])
    o_ref[...] = (_sigmoid(ga_ref[...]) * ua + _sigmoid(gb_ref[...]) * ub).astype(o_ref.dtype)


def merge(o_a, o_b, p, ga_off, gb_off, w_up_a, w_up_b):
    n, d_a = o_a.shape
    d = w_up_a.shape[1]
    tm = _pick(n, 1376)
    tn = SEG
    return pl.pallas_call(
        _merge_kernel,
        grid=(n // tm, d // tn),
        in_specs=[pl.BlockSpec((tm, d_a), lambda i, j: (i, 0)),
                  pl.BlockSpec((tm, o_b.shape[1]), lambda i, j: (i, 0)),
                  pl.BlockSpec((tm, tn), lambda i, j: (i, ga_off // tn + j)),
                  pl.BlockSpec((tm, tn), lambda i, j: (i, gb_off // tn + j)),
                  pl.BlockSpec((d_a, tn), lambda i, j: (0, j)),
                  pl.BlockSpec((w_up_b.shape[0], tn), lambda i, j: (0, j))],
        out_specs=pl.BlockSpec((tm, tn), lambda i, j: (i, j)),
        out_shape=jax.ShapeDtypeStruct((n, d), BF16),
        compiler_params=_params("parallel", "arbitrary"),
        name="merge",
    )(o_a, o_b, p, p, w_up_a, w_up_b)


def _wo_kernel(m_ref, x_ref, w_ref, o_ref):
    o_ref[...] = x_ref[...] + _dot(m_ref[...], w_ref[...])


def out_proj(m, x, w_o):
    n, d = x.shape
    tm = _pick(n, 1376)
    tn = SEG
    return pl.pallas_call(
        _wo_kernel,
        grid=(n // tm, d // tn),
        in_specs=[pl.BlockSpec((tm, d), lambda i, j: (i, 0)),
                  pl.BlockSpec((tm, tn), lambda i, j: (i, j)),
                  pl.BlockSpec((d, tn), lambda i, j: (0, j))],
        out_specs=pl.BlockSpec((tm, tn), lambda i, j: (i, j)),
        out_shape=jax.ShapeDtypeStruct((n, d), F32),
        compiler_params=_params("parallel", "arbitrary"),
        name="out_proj",
    )(m, x, w_o)


def _router_kernel(x_ref, g_ref, w_ref, bias_ref, h_ref, r_ref):
    x = x_ref[...]
    h = x * lax.rsqrt(jnp.mean(x * x, -1, keepdims=True) + NORM_EPS) * g_ref[...]
    h_ref[...] = h.astype(h_ref.dtype)
    logits = _dot(h, w_ref[...], precision=HIGHEST)
    biased = logits + bias_ref[...]
    lane_i = lax.broadcasted_iota(jnp.int32, logits.shape, 1)
    lane = lane_i.astype(F32)
    lane_group = (lane_i // EXPERTS_PER_GROUP).astype(F32)
    is_g = (lane_i >= N_EXPERTS) & (lane_i < N_EXPERTS + N_GROUPS)
    big = float(1 << 20)

    def first_argmax(vals):
        mx = jnp.max(vals, axis=-1, keepdims=True)
        return jnp.min(jnp.where(vals == mx, lane, big), axis=-1, keepdims=True)

    def pick(vals, idx):
        return jnp.sum(jnp.where(lane == idx, vals, 0.0), axis=-1, keepdims=True)

    g_lane = first_argmax(jnp.where(is_g, biased, -jnp.inf))
    g_sel = g_lane - float(N_EXPERTS)
    g_max = jnp.max(jnp.where(is_g, logits, -jnp.inf), axis=-1, keepdims=True)
    g_exp = jnp.where(is_g, jnp.exp(logits - g_max), 0.0)
    p_sel = pick(g_exp, g_lane) / jnp.sum(g_exp, axis=-1, keepdims=True)

    in_grp = (lane_i < N_EXPERTS) & (lane_group == g_sel)
    e_biased = jnp.where(in_grp, biased, -jnp.inf)
    i1 = first_argmax(e_biased)
    i2 = first_argmax(jnp.where(lane == i1, -jnp.inf, e_biased))
    l1, l2 = pick(logits, i1), pick(logits, i2)
    mx = jnp.maximum(l1, l2)
    e1, e2 = jnp.exp(l1 - mx), jnp.exp(l2 - mx)
    w1 = e1 / (e1 + e2) * p_sel
    w2 = e2 / (e1 + e2) * p_sel
    r_ref[...] = jnp.where(lane_i == 0, i1, jnp.where(lane_i == 1, i2, jnp.where(lane_i == 2, w1,
                                                                                   jnp.where(lane_i == 3, w2, 0.0))))


def router(x, g, w_router, bias_router):
    n, d = x.shape
    tm = _pick(n, 688)
    return pl.pallas_call(
        _router_kernel,
        grid=(n // tm,),
        in_specs=[pl.BlockSpec((tm, d), lambda i: (i, 0)),
                  pl.BlockSpec((1, d), lambda i: (0, 0)),
                  pl.BlockSpec((d, LANES), lambda i: (0, 0)),
                  pl.BlockSpec((1, LANES), lambda i: (0, 0))],
        out_specs=[pl.BlockSpec((tm, d), lambda i: (i, 0)), pl.BlockSpec((tm, LANES), lambda i: (i, 0))],
        out_shape=[jax.ShapeDtypeStruct((n, d), BF16), jax.ShapeDtypeStruct((n, LANES), F32)],
        compiler_params=_params("parallel"),
        name="router",
    )(x, g, w_router, bias_router)


def _expert_kernel(te_ref, nt_ref, x_ref, gate_ref, wg_ref, wu_ref, wd_ref, o_ref):
    del te_ref

    @pl.when(pl.program_id(0) < nt_ref[0])
    def _():
        x = x_ref[...]
        hg = _dot(x, wg_ref[...])
        hu = _dot(x, wu_ref[...])
        act = hg * _sigmoid(hg) * hu * gate_ref[...]
        o_ref[...] = _dot(act.astype(BF16), wd_ref[...])

    @pl.when(pl.program_id(0) >= nt_ref[0])
    def _():
        o_ref[...] = jnp.zeros(o_ref.shape, o_ref.dtype)


def experts(tile_expert, n_tiles_used, xg, gates, w_gate, w_up, w_down):
    m_pad, d = xg.shape
    ff = w_gate.shape[-1]
    tm = EXPERT_TILE
    grid_spec = pltpu.PrefetchScalarGridSpec(
        num_scalar_prefetch=2,
        grid=(m_pad // tm,),
        in_specs=[pl.BlockSpec((tm, d), lambda i, te, nt: (i, 0)),
                  pl.BlockSpec((tm, 1), lambda i, te, nt: (i, 0)),
                  pl.BlockSpec((None, d, ff), lambda i, te, nt: (te[i], 0, 0)),
                  pl.BlockSpec((None, d, ff), lambda i, te, nt: (te[i], 0, 0)),
                  pl.BlockSpec((None, ff, d), lambda i, te, nt: (te[i], 0, 0))],
        out_specs=pl.BlockSpec((tm, d), lambda i, te, nt: (i, 0)),
    )
    return pl.pallas_call(
        _expert_kernel,
        grid_spec=grid_spec,
        out_shape=jax.ShapeDtypeStruct((m_pad, d), F32),
        compiler_params=_params("arbitrary"),
        name="experts",
    )(tile_expert, n_tiles_used, xg, gates, w_gate, w_up, w_down)


def _combine_kernel(x_ref, e0_ref, e1_ref, g_ref, o_ref):
    x = x_ref[...] + (e0_ref[...] + e1_ref[...])
    o_ref[...] = x * lax.rsqrt(jnp.mean(x * x, -1, keepdims=True) + NORM_EPS) * g_ref[...]


def combine(x, e0, e1, g):
    n, d = x.shape
    tm = _pick(n, 688)
    row = pl.BlockSpec((tm, d), lambda i: (i, 0))
    return pl.pallas_call(
        _combine_kernel,
        grid=(n // tm,),
        in_specs=[row, row, row, pl.BlockSpec((1, d), lambda i: (0, 0))],
        out_specs=row,
        out_shape=jax.ShapeDtypeStruct((n, d), F32),
        compiler_params=_params("parallel"),
        name="combine",
    )(x, e0, e1, g)


def _pad_cols(w, width):
    return jnp.pad(w, ((0, 0), (0, width - w.shape[1])))


def _pad_rows(w, height):
    return jnp.pad(w, ((0, height - w.shape[0]), (0, 0)))


def _swap_halves(w):
    half = w.shape[-1] // 2
    return jnp.concatenate([w[..., half:], w[..., :half]], axis=-1)


def _block_diag_pairs(w):
    h, a, b = w.shape
    z = jnp.zeros((h // 2, a, b), w.dtype)
    top = jnp.concatenate([w[0::2], z], axis=-1)
    bot = jnp.concatenate([z, w[1::2]], axis=-1)
    return jnp.concatenate([top, bot], axis=1)


def _rope_tables(pos):
    half = QK_ROPE // 2
    inv = ROPE_THETA ** (-jnp.arange(half, dtype=F32) / half)
    ang = pos.astype(F32)[:, None] * inv[None, :]
    cos, sin = jnp.cos(ang), jnp.sin(ang)
    cos_t = _pad_cols(jnp.concatenate([cos, cos], -1), LANES)
    sin_t = _pad_cols(jnp.concatenate([-sin, sin], -1), LANES)
    return cos_t, sin_t


def _layer_weights(l, w_in, rwkv_mu, rwkv_w0, rwkv_w2, rwkv_a0, rwkv_a2, rwkv_g2, rwkv_k_k, rwkv_k_a, rwkv_r_k,
                   rwkv_ln_g, rwkv_ln_b, mla_q_norm_g, mla_w_uq, mla_kv_norm_g, mla_w_uk, mla_w_uv, w_up_a,
                   w_up_b, w_o, norm_mix_g):
    d = w_in.shape[1]
    d_a = rwkv_w0.shape[1]
    n_heads = mla_w_uk.shape[2]
    off_kv, off_kr = Q_LORA, Q_LORA + KV_LORA
    off_rw = off_kr + QK_ROPE
    off_ga = off_rw + 3 * d_a + DECAY_LORA + AAA_LORA + GATE_LORA
    off_gb = off_ga + d
    w = w_in[l]
    rw = w[:, off_rw:off_ga]

    def lora_layout(m):
        wl = m[:, 3 * d_a:3 * d_a + DECAY_LORA]
        al = m[:, 3 * d_a + DECAY_LORA:3 * d_a + DECAY_LORA + AAA_LORA]
        gl = m[:, 3 * d_a + DECAY_LORA + AAA_LORA:]
        return jnp.concatenate([_pad_cols(wl, LANES), _pad_cols(al, LANES), gl], axis=1)

    kr = w[:, off_kr:off_rw]
    kv_seg = jnp.concatenate([w[:, off_kv:off_kr], _pad_cols(kr, LANES), _pad_cols(_swap_halves(kr), LANES)], axis=1)
    w_all = jnp.concatenate([rw[:, :3 * d_a], lora_layout(rw), w[:, off_ga:off_gb], w[:, off_gb:],
                             w[:, :Q_LORA], kv_seg], axis=1).astype(BF16)
    mu = rwkv_mu[l][None]
    mu_all = jnp.concatenate([mu[:, :3 * d_a], lora_layout(mu)], axis=1)

    uq = mla_w_uq[l].reshape(Q_LORA, n_heads, QK_NOPE + QK_ROPE)
    w_nope = uq[:, :, :QK_NOPE].reshape(Q_LORA, n_heads * QK_NOPE).astype(BF16)
    rope_cols = uq[:, :, QK_NOPE:]
    pad_heads = lambda m: jnp.pad(m, ((0, 0), (0, 0), (0, LANES - QK_ROPE))).reshape(Q_LORA, n_heads * LANES)
    w_rope = pad_heads(rope_cols).astype(BF16)
    w_rope_sw = pad_heads(_swap_halves(rope_cols)).astype(BF16)
    w_uk_pairs = _block_diag_pairs(jnp.transpose(mla_w_uk[l], (1, 2, 0))).astype(BF16)
    w_uv_pairs = _block_diag_pairs(jnp.transpose(mla_w_uv[l], (1, 0, 2))).astype(BF16)

    return dict(
        w_all=w_all, mu_all=mu_all, norm_mix_g=norm_mix_g[l][None],
        ga_off=3 * d_a + SEG, gb_off=3 * d_a + SEG + d,
        w0=rwkv_w0[l][None], a0=rwkv_a0[l][None], k_k=rwkv_k_k[l][None], k_a=rwkv_k_a[l][None],
        w2=_pad_rows(rwkv_w2[l], LANES).astype(BF16), a2=_pad_rows(rwkv_a2[l], LANES).astype(BF16),
        g2=rwkv_g2[l].astype(BF16),
        r_k=rwkv_r_k[l].reshape(1, d_a), ln_g=rwkv_ln_g[l][None], ln_b=rwkv_ln_b[l][None],
        g_q=mla_q_norm_g[l][None], g_kv=mla_kv_norm_g[l][None],
        w_nope=w_nope, w_rope=w_rope, w_rope_sw=w_rope_sw, w_uk_pairs=w_uk_pairs, w_uv_pairs=w_uv_pairs,
        w_up_a=w_up_a[l].astype(BF16), w_up_b=w_up_b[l].astype(BF16), w_o=w_o[l].astype(BF16),
        n_heads=n_heads, d_a=d_a, lora_layout=lora_layout,
    )


def _state_to_pairs(s):
    b, h = s.shape[:2]
    z = jnp.zeros((b, h // 2, HEAD_A, HEAD_A), s.dtype)
    top = jnp.concatenate([s[:, 0::2], z], axis=-1)
    bot = jnp.concatenate([z, s[:, 1::2]], axis=-1)
    return jnp.concatenate([top, bot], axis=2)


def _pairs_to_state(sp):
    b, hp = sp.shape[:2]
    even = sp[:, :, :HEAD_A, :HEAD_A]
    odd = sp[:, :, HEAD_A:, HEAD_A:]
    return jnp.stack([even, odd], axis=2).reshape(b, 2 * hp, HEAD_A, HEAD_A)


def _token_mixer(x3, pos, shift_prev, wkv_prev, lw, attend):
    b, t, d = x3.shape
    n = b * t
    d_a = lw["d_a"]
    x2 = x3.reshape(n, d)
    p = rms_matmul(x2, lw["norm_mix_g"], lw["w_all"])
    p3 = p.reshape(b, t, p.shape[1])

    if t >= 344:
        bb, tt = 1, _pick(t, 344)
    else:
        bb, tt = _pick(b, max(1, 256 // t), 1), t
    shift_cols = jnp.concatenate([shift_prev[:, :3 * d_a], lw["lora_layout"](shift_prev)], axis=1)[:, None, :]
    r, lwd, k_mod, v, kk, bvec, g = rwkv_prepare(p3, shift_cols, lw["mu_all"], lw["w0"], lw["a0"], lw["k_k"],
                                                   lw["k_a"], lw["w2"], lw["a2"], lw["g2"], bb, tt)
    chunk = _pick(t, SCAN_CHUNK)
    y, s_new = rwkv_scan(r, lwd, k_mod, v, kk, bvec, _state_to_pairs(wkv_prev), chunk)
    flat = lambda a: a.reshape(n, d_a)
    o_a = rwkv_out(flat(y), flat(r), flat(k_mod), flat(v), flat(g), lw["ln_g"], lw["ln_b"], lw["r_k"])

    cos_t, sin_t = _rope_tables(pos)
    qf, kf, ckv, krope = mla_project(p3, cos_t, sin_t, lw["g_q"], lw["g_kv"], lw["w_nope"], lw["w_rope"],
                                     lw["w_rope_sw"], lw["w_uk_pairs"], lw["n_heads"], bb, tt)
    o_b = attend(qf, kf).reshape(n, -1)

    m = merge(o_a, o_b, p, lw["ga_off"], lw["gb_off"], lw["w_up_a"], lw["w_up_b"])
    x_new = out_proj(m, x2, lw["w_o"])

    last = p3[:, -1, :]
    shift_new = jnp.concatenate([last[:, :3 * d_a],
                                 last[:, 3 * d_a:3 * d_a + DECAY_LORA],
                                 last[:, 3 * d_a + LANES:3 * d_a + LANES + AAA_LORA],
                                 last[:, 3 * d_a + 2 * LANES:3 * d_a + SEG]], axis=1)
    return x_new, (ckv, krope, _pairs_to_state(s_new), shift_new)


def _moe(x_rows, norm_g, router_group_w, router_group_b, router_expert_w, router_expert_b,
         w_gate, w_up, w_down, final_g):
    d = x_rows[0].shape[1]
    w_router = _pad_cols(jnp.concatenate([router_expert_w, router_group_w], axis=1), LANES)
    b_router = _pad_cols(jnp.concatenate([router_expert_b, router_group_b])[None], LANES)
    hs, routes = zip(*[router(x, norm_g, w_router, b_router) for x in x_rows])
    h = jnp.concatenate(hs, axis=0)
    route = jnp.concatenate(routes, axis=0)
    n = h.shape[0]
    eid = route[:, :2].astype(jnp.int32).reshape(-1)
    gate = route[:, 2:4].reshape(-1)

    tm = EXPERT_TILE
    n_tiles = -(-(2 * n + N_EXPERTS * (tm - 1)) // tm)
    m_pad = n_tiles * tm
    order = jnp.argsort(eid, stable=True)
    counts = jnp.zeros((N_EXPERTS,), jnp.int32).at[eid].add(1)
    padded = ((counts + tm - 1) // tm) * tm
    seg_start = jnp.cumsum(padded) - padded
    cnt_start = jnp.cumsum(counts) - counts
    sorted_e = eid[order]
    dest_sorted = seg_start[sorted_e] + (jnp.arange(2 * n, dtype=jnp.int32) - cnt_start[sorted_e])
    dest = jnp.zeros((2 * n,), jnp.int32).at[order].set(dest_sorted)
    row_token = jnp.zeros((m_pad,), jnp.int32).at[dest].set(jnp.arange(2 * n, dtype=jnp.int32) // 2)
    row_gate = jnp.zeros((m_pad,), F32).at[dest].set(gate)
    seg_end = jnp.cumsum(padded)
    tile_expert = jnp.minimum(jnp.searchsorted(seg_end, jnp.arange(n_tiles, dtype=jnp.int32) * tm, side="right"),
                              N_EXPERTS - 1).astype(jnp.int32)
    n_used = (seg_end[-1] // tm).astype(jnp.int32).reshape(1)

    xg = jnp.take(h, row_token, axis=0)
    eo = experts(tile_expert, n_used, xg, row_gate[:, None], w_gate.astype(BF16), w_up.astype(BF16),
                 w_down.astype(BF16))
    dest2 = dest.reshape(n, 2)
    e0 = jnp.take(eo, dest2[:, 0], axis=0)
    e1 = jnp.take(eo, dest2[:, 1], axis=0)

    outs, start = [], 0
    for x in x_rows:
        ni = x.shape[0]
        outs.append(combine(x, e0[start:start + ni], e1[start:start + ni], final_g))
        start += ni
    return outs


def kernel(x_prompt, x_sample, cache_ckv, cache_krope, state_wkv, state_shift, page_table, meta_tokens, norm_mix_g, w_in, rwkv_mu, rwkv_w0, rwkv_w2, rwkv_a0, rwkv_a2, rwkv_g2, rwkv_k_k, rwkv_k_a, rwkv_r_k, rwkv_ln_g, rwkv_ln_b, mla_q_norm_g, mla_w_uq, mla_kv_norm_g, mla_w_uk, mla_w_uv, w_up_a, w_up_b, w_o, norm_ffn_g, router_group_w, router_group_b, router_expert_w, router_expert_b, expert_w_gate, expert_w_up, expert_w_down, norm_final_g):
    depth = w_in.shape[0]
    assert depth == 1, "the final norm is fused into the MoE combine, which assumes a single layer"
    b, s, d = x_prompt.shape
    db, ds, _ = x_sample.shape
    past = page_table.shape[1] * PAGE_SIZE
    t = N_META + s
    n_heads_a = state_wkv.shape[2]
    rwkv_cols = state_shift.shape[-1]

    xp = jnp.concatenate([jnp.broadcast_to(meta_tokens[None], (b, N_META, d)), x_prompt], axis=1)
    l = 0
    lw = _layer_weights(l, w_in, rwkv_mu, rwkv_w0, rwkv_w2, rwkv_a0, rwkv_a2, rwkv_g2, rwkv_k_k, rwkv_k_a,
                        rwkv_r_k, rwkv_ln_g, rwkv_ln_b, mla_q_norm_g, mla_w_uq, mla_kv_norm_g, mla_w_uk,
                        mla_w_uv, w_up_a, w_up_b, w_o, norm_mix_g)

    xp1, st_p = _token_mixer(xp, jnp.arange(t), jnp.zeros((b, rwkv_cols), F32),
                             jnp.zeros((b, n_heads_a, HEAD_A, HEAD_A), F32), lw,
                             lambda qf, kf: attention_prompt(qf, kf, lw["w_uv_pairs"]))
    xs1, st_s = _token_mixer(x_sample, past + jnp.arange(ds), state_shift[l], state_wkv[l], lw,
                             lambda qf, kf: attention_sample(qf, kf, cache_ckv, cache_krope, l,
                                                             page_table, lw["w_uv_pairs"]))

    yp, ys = _moe([xp1, xs1], norm_ffn_g[l][None], router_group_w[l], router_group_b[l], router_expert_w[l],
                  router_expert_b[l], expert_w_gate[l], expert_w_up[l], expert_w_down[l], norm_final_g[None])
    y_prompt = yp.reshape(b, t, d)[:, N_META:]
    y_sample = ys.reshape(db, ds, d)
    return (y_prompt, y_sample,
            st_p[0][None], st_p[1][None], st_p[2][None], st_p[3][None],
            st_s[0][None], st_s[1][None], st_s[2][None], st_s[3][None])
```

```python
import functools

import jax
import jax.numpy as jnp
import numpy as np
from jax import lax
from jax.experimental import pallas as pl
from jax.experimental.pallas import tpu as pltpu

F32 = jnp.float32
BF16 = jnp.bfloat16
HIGHEST = lax.Precision.HIGHEST

N_META = 16
NORM_EPS = 1e-6
NEG = -1e30
HEAD_A = 64
DECAY_LORA = 96
AAA_LORA = 96
GATE_LORA = 256
LNX_EPS = 64e-5
QK_NOPE = 64
QK_ROPE = 32
V_HEAD = 64
Q_LORA = 512
KV_LORA = 256
ROPE_THETA = 10000.0
PAGE_SIZE = 128
N_GROUPS = 4
EXPERTS_PER_GROUP = 8
N_EXPERTS = N_GROUPS * EXPERTS_PER_GROUP

LANES = 128
HEAD_PAIR = 2 * HEAD_A
SEG = 512
LATENT_PAD = KV_LORA + LANES
VMEM_LIMIT = 56 * 1024 * 1024
SCAN_CHUNK = 16
EXPERT_TILE = 256
PAGES_PER_STEP = 16


def _pick(n, target, mult=8):
    best = None
    for d in range(mult, min(n, target) + 1, mult):
        if n % d == 0:
            best = d
    return best if best is not None else n


def _params(*sem):
    return pltpu.CompilerParams(dimension_semantics=sem, vmem_limit_bytes=VMEM_LIMIT)


def _dot(a, b, **kw):
    return jnp.dot(a, b, preferred_element_type=F32, **kw)


def _dot_nt(a, b):
    return lax.dot_general(a, b, (((1,), (1,)), ((), ())), preferred_element_type=F32)


def _dot_tn(a, b):
    return lax.dot_general(a, b, (((0,), (0,)), ((), ())), preferred_element_type=F32)


def _sigmoid(x):
    return 1.0 / (1.0 + jnp.exp(-x))


def _head_ones(scale=1.0):
    r = lax.broadcasted_iota(jnp.int32, (LANES, LANES), 0) // HEAD_A
    c = lax.broadcasted_iota(jnp.int32, (LANES, LANES), 1) // HEAD_A
    return jnp.where(r == c, scale, 0.0).astype(F32)


def _rms_mm_kernel(x_ref, g_ref, w_ref, o_ref, h_ref):
    @pl.when(pl.program_id(1) == 0)
    def _():
        x = x_ref[...]
        ms = jnp.mean(x * x, axis=-1, keepdims=True)
        h_ref[...] = (x * lax.rsqrt(ms + NORM_EPS) * g_ref[...]).astype(BF16)

    o_ref[...] = _dot(h_ref[...], w_ref[...])


def rms_matmul(x, g, w):
    n, k = x.shape
    m = w.shape[1]
    tm = _pick(n, 1376)
    tn = SEG
    return pl.pallas_call(
        _rms_mm_kernel,
        grid=(n // tm, m // tn),
        in_specs=[pl.BlockSpec((tm, k), lambda i, j: (i, 0)),
                  pl.BlockSpec((1, k), lambda i, j: (0, 0)),
                  pl.BlockSpec((k, tn), lambda i, j: (0, j))],
        out_specs=pl.BlockSpec((tm, tn), lambda i, j: (i, j)),
        out_shape=jax.ShapeDtypeStruct((n, m), F32),
        scratch_shapes=[pltpu.VMEM((tm, k), BF16)],
        compiler_params=_params("parallel", "arbitrary"),
        name="rms_matmul",
    )(x, g, w)


def _prepare_kernel(pr_ref, pk_ref, pv_ref, pl_ref, sh_ref, mu_ref, w0_ref, a0_ref, kk_ref, ka_ref, rk_ref,
                    w2_ref, a2_ref, g2_ref,
                    rt_out, qu_out, kh_out, bh_out, kt_out, bt_out, v_out, gam_out, bonus_out, g_out,
                    carry_ref, *, chunk):
    tc = pr_ref.shape[-1]
    ct = pl.program_id(2)

    @pl.when(pl.program_id(1) == 0)
    def _():
        carry_ref[ct] = sh_ref[...]

    def shifted(ref, lo, width):
        x = ref[...]
        prev_row = carry_ref[ct, :, :, lo:lo + width]
        t_idx = lax.broadcasted_iota(jnp.int32, x.shape, 1)
        prev = jnp.where(t_idx == 0, prev_row, pltpu.roll(x, 1, axis=1))
        carry_ref[ct, :, :, lo:lo + width] = x[:, x.shape[1] - 1:, :]
        xs = x + (prev - x) * mu_ref[:, lo:lo + width]
        return xs.reshape(x.shape[0] * x.shape[1], width)

    bb, tt, _ = pr_ref.shape
    rows = bb * tt
    r = shifted(pr_ref, 0, tc)
    k = shifted(pk_ref, tc, tc)
    v = shifted(pv_ref, 2 * tc, tc)
    lora = shifted(pl_ref, 3 * tc, SEG)
    wl, al, gl = lora[:, :LANES], lora[:, LANES:2 * LANES], lora[:, 2 * LANES:]

    z = -(w0_ref[...] + _dot(jnp.tanh(wl).astype(BF16), w2_ref[...]))
    softplus = jnp.maximum(z, 0.0) + jnp.log(1.0 + jnp.exp(-jnp.abs(z)))
    lw = -jnp.exp(-softplus - 0.5)
    a = _sigmoid(a0_ref[...] + _dot(al.astype(BF16), a2_ref[...]))
    g = _dot(_sigmoid(gl).astype(BF16), g2_ref[...])

    kkf = k * kk_ref[...]
    ones = _head_ones()
    per_head = lambda x: jnp.concatenate(
        [_dot(x[:, j * LANES:(j + 1) * LANES], ones, precision=HIGHEST) for j in range(tc // LANES)], axis=-1)
    kk = kkf * lax.rsqrt(jnp.maximum(per_head(jnp.square(kkf)), 1e-24))
    k_mod = k * (1.0 + (a - 1.0) * ka_ref[...])
    b = kk * a
    bonus = per_head(r * k_mod * rk_ref[...]) * v

    pos = lax.broadcasted_iota(jnp.int32, (rows, tc), 0) % chunk
    cum = lw
    sh = 1
    while sh < chunk:
        cum = cum + jnp.where(pos >= sh, pltpu.roll(cum, sh, axis=0), 0.0)
        sh *= 2
    total = jnp.where(pos == chunk - 1, cum, 0.0)
    sh = 1
    while sh < chunk:
        total = total + pltpu.roll(total, rows - sh, axis=0)
        sh *= 2
    e_neg = jnp.exp(-cum)
    e_tail = jnp.exp(total - cum)

    def st(ref, val):
        ref[...] = val.reshape(bb, tt, tc).astype(ref.dtype)

    st(rt_out, r * jnp.exp(cum))
    st(qu_out, kk * jnp.exp(cum - lw))
    st(kh_out, k_mod * e_neg)
    st(bh_out, b * e_neg)
    st(kt_out, k_mod * e_tail)
    st(bt_out, b * e_tail)
    st(v_out, v)
    st(bonus_out, bonus)
    st(g_out, g)
    gam_out[...] = jnp.exp(total).reshape(bb, tt // chunk, chunk, tc)[:, :, 0:1, :]


def rwkv_prepare(p3, shift4, mu3, w0, a0, k_k, k_a, r_k, w2, a2, g2, bb, tt, chunk):
    b, t, _ = p3.shape
    d_a = w0.shape[-1]
    tc = SEG
    n_ct = d_a // tc
    width = 3 * tc + SEG
    row = lambda part: pl.BlockSpec((bb, tt, tc), lambda i, j, c, part=part: (i, j, part * n_ct + c))
    vec = pl.BlockSpec((1, tc), lambda i, j, c: (0, c))
    mat = lambda a: pl.BlockSpec((a.shape[0], tc), lambda i, j, c: (0, c))
    out_spec = pl.BlockSpec((bb, tt, tc), lambda i, j, c: (i, j, c))
    seq = lambda dt: jax.ShapeDtypeStruct((b, t, d_a), dt)
    return pl.pallas_call(
        functools.partial(_prepare_kernel, chunk=chunk),
        grid=(b // bb, t // tt, n_ct),
        in_specs=[row(0), row(1), row(2),
                  pl.BlockSpec((bb, tt, SEG), lambda i, j, c: (i, j, 3 * d_a // SEG)),
                  pl.BlockSpec((bb, None, 1, width), lambda i, j, c: (i, c, 0, 0)),
                  pl.BlockSpec((None, 1, width), lambda i, j, c: (c, 0, 0)),
                  vec, vec, vec, vec, vec, mat(w2), mat(a2), mat(g2)],
        out_specs=[out_spec] * 7
                  + [pl.BlockSpec((bb, tt // chunk, 1, tc), lambda i, j, c: (i, j, 0, c)), out_spec, out_spec],
        out_shape=[seq(BF16)] * 7 + [jax.ShapeDtypeStruct((b, t // chunk, 1, d_a), F32), seq(F32), seq(F32)],
        scratch_shapes=[pltpu.VMEM((n_ct, bb, 1, width), F32)],
        compiler_params=_params("parallel", "arbitrary", "arbitrary"),
        name="rwkv_prepare",
    )(p3, p3, p3, p3, shift4, mu3, w0, a0, k_k, k_a, r_k, w2, a2, g2)


def _scan_kernel(rt_ref, qu_ref, kh_ref, bh_ref, kt_ref, bt_ref, v_ref, gam_ref, s0_ref, y_ref, sf_ref, st_ref,
                 *, chunk):
    c_idx = pl.program_id(1)
    nb, n_pairs = st_ref.shape[:2]
    chains = [(bi, j) for bi in range(nb) for j in range(n_pairs)]
    c2x = 2 * chunk

    @pl.when(c_idx == 0)
    def _():
        zero = jnp.zeros((HEAD_A, HEAD_A), F32)
        for bi, j in chains:
            top = jnp.concatenate([s0_ref[bi, 2 * j], zero], axis=1)
            bot = jnp.concatenate([zero, s0_ref[bi, 2 * j + 1]], axis=1)
            st_ref[bi, j] = jnp.concatenate([top, bot], axis=0)

    r2 = lax.broadcasted_iota(jnp.int32, (c2x, c2x), 0)
    c2 = lax.broadcasted_iota(jnp.int32, (c2x, c2x), 1)
    strict = c2 < r2
    incl = c2 <= r2
    head0 = lax.broadcasted_iota(jnp.int32, (chunk, LANES), 1) < HEAD_A

    def stack2(ref, bi, j):
        x = ref[bi, :, j * LANES:(j + 1) * LANES].astype(F32)
        return jnp.concatenate([jnp.where(head0, x, 0.0), jnp.where(head0, 0.0, x)], axis=0).astype(BF16)

    each = lambda f, *lists: [f(*args) for args in zip(*lists)]
    s_prev = [st_ref[bi, j] for bi, j in chains]
    gam = [gam_ref[bi, 0, :, j * LANES:(j + 1) * LANES] for bi, j in chains]
    q_u, q_r, k_h, b_h, k_t, b_t, v2 = [[stack2(ref, bi, j) for bi, j in chains]
                                        for ref in (qu_ref, rt_ref, kh_ref, bh_ref, kt_ref, bt_ref, v_ref)]
    qq = each(lambda a, b: jnp.concatenate([a, b], axis=0), q_u, q_r)
    s_bf = [s.astype(BF16) for s in s_prev]
    sk = each(_dot_nt, qq, k_h)
    sb = each(_dot_nt, qq, b_h)
    ss = each(_dot_nt, qq, s_bf)
    a_vk = [jnp.where(strict, m[:c2x], 0.0).astype(BF16) for m in sk]
    power = [jnp.where(strict, -m[:c2x], 0.0) for m in sb]
    a_rk = [jnp.where(incl, m[c2x:], 0.0).astype(BF16) for m in sk]
    a_rb = [jnp.where(incl, m[c2x:], 0.0).astype(BF16) for m in sb]
    x = each(lambda s, a, v: s[:c2x] + _dot(a, v), ss, a_vk, v2)
    steps = max(1, int(np.ceil(np.log2(chunk))))
    for s in range(steps):
        pb = [p.astype(BF16) for p in power]
        x = each(lambda xi, p: xi + _dot(p, xi.astype(BF16)), x, pb)
        if s + 1 < steps:
            power = each(_dot, pb, pb)
    u2 = [xi.astype(BF16) for xi in x]
    y2 = each(lambda s, ak, v, ab, u: s[c2x:] + _dot(ak, v) - _dot(ab, u), ss, a_rk, v2, a_rb, u2)
    s_new = each(lambda s, g, v, u, kt, bt: s * g + _dot_tn(jnp.concatenate([v, -u], axis=0),
                                                              jnp.concatenate([kt, bt], axis=0)),
                 s_prev, gam, v2, u2, k_t, b_t)

    for (bi, j), y, s in zip(chains, y2, s_new):
        y_ref[bi, :, j * LANES:(j + 1) * LANES] = y[:chunk] + y[chunk:]
        st_ref[bi, j] = s

    @pl.when(c_idx == pl.num_programs(1) - 1)
    def _():
        for (bi, j), s in zip(chains, s_new):
            sf_ref[bi, 2 * j] = s[:HEAD_A, :HEAD_A]
            sf_ref[bi, 2 * j + 1] = s[HEAD_A:, HEAD_A:]


def rwkv_scan(rt, qu, kh, bh, kt, bt, v, gam, s0, chunk):
    bsz, t, d_a = rt.shape
    n_pairs = d_a // LANES
    nb = 2 if bsz % 2 == 0 else 1
    seq = pl.BlockSpec((nb, chunk, d_a), lambda i, c: (i, c, 0))
    st = pl.BlockSpec((nb, 2 * n_pairs, HEAD_A, HEAD_A), lambda i, c: (i, 0, 0, 0))
    return pl.pallas_call(
        functools.partial(_scan_kernel, chunk=chunk),
        grid=(bsz // nb, t // chunk),
        in_specs=[seq] * 7 + [pl.BlockSpec((nb, 1, 1, d_a), lambda i, c: (i, c, 0, 0)), st],
        out_specs=[seq, st],
        out_shape=[jax.ShapeDtypeStruct((bsz, t, d_a), F32),
                   jax.ShapeDtypeStruct((bsz, 2 * n_pairs, HEAD_A, HEAD_A), F32)],
        scratch_shapes=[pltpu.VMEM((nb, n_pairs, LANES, LANES), F32)],
        compiler_params=_params("parallel", "arbitrary"),
        name="rwkv_scan",
    )(rt, qu, kh, bh, kt, bt, v, gam, s0)


def _rwkv_out_kernel(y_ref, bonus_ref, g_ref, lng_ref, lnb_ref, o_ref):
    mean_m = _head_ones(1.0 / HEAD_A)
    for j in range(y_ref.shape[-1] // LANES):
        sl = slice(j * LANES, (j + 1) * LANES)
        y = y_ref[:, sl]
        d = y - _dot(y, mean_m, precision=HIGHEST)
        var = _dot(d * d, mean_m, precision=HIGHEST)
        yn = d * lax.rsqrt(var + LNX_EPS) * lng_ref[:, sl] + lnb_ref[:, sl]
        o_ref[:, sl] = ((yn + bonus_ref[:, sl]) * g_ref[:, sl]).astype(o_ref.dtype)


def rwkv_out(y, bonus, g, ln_g, ln_b):
    n, d_a = y.shape
    tm = _pick(n, 688)
    row = pl.BlockSpec((tm, d_a), lambda i: (i, 0))
    vec = pl.BlockSpec((1, d_a), lambda i: (0, 0))
    return pl.pallas_call(
        _rwkv_out_kernel,
        grid=(n // tm,),
        in_specs=[row] * 3 + [vec] * 2,
        out_specs=row,
        out_shape=jax.ShapeDtypeStruct((n, d_a), BF16),
        compiler_params=_params("parallel"),
        name="rwkv_out",
    )(y, bonus, g, ln_g, ln_b)


def _mla_kernel(pq_ref, pkv_ref, cos_ref, sin_ref, gq_ref, gkv_ref, wn_ref, wr_ref, wrs_ref, wuk_ref,
                qf_ref, kf_ref, ckv_ref, kr_ref, *, scale):
    bb, tt, _ = pq_ref.shape
    rows = bb * tt
    n_heads = qf_ref.shape[1]
    cos = jnp.broadcast_to(cos_ref[...][None], (bb, tt, LANES)).reshape(rows, LANES)
    sin = jnp.broadcast_to(sin_ref[...][None], (bb, tt, LANES)).reshape(rows, LANES)

    pq = pq_ref[...].reshape(rows, Q_LORA)
    c_q = (pq * lax.rsqrt(jnp.mean(pq * pq, -1, keepdims=True) + NORM_EPS) * gq_ref[...]).astype(BF16)
    q_nope = _dot(c_q, wn_ref[...]).astype(BF16)
    q_rope = _dot(c_q, wr_ref[...])
    q_rope_sw = _dot(c_q, wrs_ref[...])
    for j in range(n_heads // 2):
        lat = _dot(q_nope[:, j * LANES:(j + 1) * LANES], wuk_ref[j]) * scale
        for hh in range(2):
            h = 2 * j + hh
            qf_ref[:, h, :, :KV_LORA] = lat[:, hh * KV_LORA:(hh + 1) * KV_LORA].reshape(bb, tt, KV_LORA).astype(BF16)
            sl = slice(h * LANES, (h + 1) * LANES)
            roped = (q_rope[:, sl] * cos + q_rope_sw[:, sl] * sin) * scale
            qf_ref[:, h, :, KV_LORA:] = roped.reshape(bb, tt, LANES).astype(BF16)

    pkv = pkv_ref[...].reshape(rows, SEG)
    kv = pkv[:, :KV_LORA]
    ckv = kv * lax.rsqrt(jnp.mean(kv * kv, -1, keepdims=True) + NORM_EPS) * gkv_ref[...]
    krope = pkv[:, KV_LORA:KV_LORA + LANES] * cos + pkv[:, KV_LORA + LANES:] * sin
    ckv_ref[...] = ckv.reshape(bb, tt, KV_LORA)
    kr_ref[...] = krope[:, :QK_ROPE].reshape(bb, tt, QK_ROPE)
    kf_ref[:, :, :KV_LORA] = ckv.reshape(bb, tt, KV_LORA).astype(BF16)
    kf_ref[:, :, KV_LORA:] = krope.reshape(bb, tt, LANES).astype(BF16)


def mla_project(p3, cos_t, sin_t, g_q, g_kv, w_nope, w_rope, w_rope_sw, w_uk_pairs, n_heads, bb, tt):
    b, t, width = p3.shape
    q_blk = (width - 2 * SEG) // SEG
    full = lambda a: pl.BlockSpec(a.shape, lambda i, j: (0,) * a.ndim)
    scale = float((QK_NOPE + QK_ROPE) ** -0.5)
    return pl.pallas_call(
        functools.partial(_mla_kernel, scale=scale),
        grid=(b // bb, t // tt),
        in_specs=[pl.BlockSpec((bb, tt, SEG), lambda i, j: (i, j, q_blk)),
                  pl.BlockSpec((bb, tt, SEG), lambda i, j: (i, j, q_blk + 1)),
                  pl.BlockSpec((tt, LANES), lambda i, j: (j, 0)),
                  pl.BlockSpec((tt, LANES), lambda i, j: (j, 0)),
                  full(g_q), full(g_kv), full(w_nope), full(w_rope), full(w_rope_sw), full(w_uk_pairs)],
        out_specs=[pl.BlockSpec((bb, n_heads, tt, LATENT_PAD), lambda i, j: (i, 0, j, 0)),
                   pl.BlockSpec((bb, tt, LATENT_PAD), lambda i, j: (i, j, 0)),
                   pl.BlockSpec((bb, tt, KV_LORA), lambda i, j: (i, j, 0)),
                   pl.BlockSpec((bb, tt, QK_ROPE), lambda i, j: (i, j, 0))],
        out_shape=[jax.ShapeDtypeStruct((b, n_heads, t, LATENT_PAD), BF16),
                   jax.ShapeDtypeStruct((b, t, LATENT_PAD), BF16),
                   jax.ShapeDtypeStruct((b, t, KV_LORA), F32),
                   jax.ShapeDtypeStruct((b, t, QK_ROPE), F32)],
        compiler_params=_params("parallel", "parallel"),
        name="mla_project",
    )(p3, p3, cos_t, sin_t, g_q, g_kv, w_nope, w_rope, w_rope_sw, w_uk_pairs)


def _value_up(o_lat, wuv_ref, store):
    n_heads = o_lat.shape[0]
    for j in range(n_heads // 2):
        lhs = jnp.concatenate([o_lat[2 * j], o_lat[2 * j + 1]], axis=-1).astype(BF16)
        store(j, _dot(lhs, wuv_ref[j]))


def _attn_prompt_kernel(q_ref, k_ref, wuv_ref, o_ref, m_ref, l_ref, acc_ref, *, tq):
    qi, ki = pl.program_id(1), pl.program_id(2)
    n_heads = q_ref.shape[1]
    rows = n_heads * tq

    @pl.when(ki == 0)
    def _():
        m_ref[...] = jnp.full(m_ref.shape, NEG, F32)
        l_ref[...] = jnp.zeros(l_ref.shape, F32)
        acc_ref[...] = jnp.zeros(acc_ref.shape, F32)

    def step(diagonal):
        q = q_ref[0].reshape(rows, LATENT_PAD)
        k = k_ref[0]
        s = _dot_nt(q, k)
        if diagonal:
            causal = (lax.broadcasted_iota(jnp.int32, (tq, tq), 1) <= lax.broadcasted_iota(jnp.int32, (tq, tq), 0))
            s = jnp.where(causal[None], s.reshape(n_heads, tq, tq), NEG).reshape(rows, tq)
        m_prev = m_ref[...]
        m_new = jnp.maximum(m_prev, jnp.max(s, axis=-1, keepdims=True))
        corr = jnp.exp(m_prev - m_new)
        p = jnp.exp(s - m_new)
        l_ref[...] = l_ref[...] * corr + jnp.sum(p, axis=-1, keepdims=True)
        acc_ref[...] = acc_ref[...] * corr + _dot(p.astype(BF16), k[:, :KV_LORA])
        m_ref[...] = m_new

    @pl.when(ki < qi)
    def _():
        step(False)

    @pl.when(ki == qi)
    def _():
        step(True)
        o_lat = (acc_ref[...] / l_ref[...]).reshape(n_heads, tq, KV_LORA)

        def store(j, val):
            o_ref[0, :, j * LANES:(j + 1) * LANES] = val.astype(o_ref.dtype)
        _value_up(o_lat, wuv_ref, store)


def attention_prompt(qf, kf, w_uv_pairs):
    b, n_heads, t, _ = qf.shape
    tq = _pick(t, 344)
    nq = t // tq
    return pl.pallas_call(
        functools.partial(_attn_prompt_kernel, tq=tq),
        grid=(b, nq, nq),
        in_specs=[pl.BlockSpec((1, n_heads, tq, LATENT_PAD), lambda i, q, k: (i, 0, q, 0)),
                  pl.BlockSpec((1, tq, LATENT_PAD), lambda i, q, k: (i, jnp.minimum(k, q), 0)),
                  pl.BlockSpec(w_uv_pairs.shape, lambda i, q, k: (0, 0, 0))],
        out_specs=pl.BlockSpec((1, tq, n_heads * V_HEAD), lambda i, q, k: (i, q, 0)),
        out_shape=jax.ShapeDtypeStruct((b, t, n_heads * V_HEAD), BF16),
        scratch_shapes=[pltpu.VMEM((n_heads * tq, 1), F32), pltpu.VMEM((n_heads * tq, 1), F32),
                        pltpu.VMEM((n_heads * tq, KV_LORA), F32)],
        compiler_params=_params("parallel", "parallel", "arbitrary"),
        name="attention_prompt",
    )(qf, kf, w_uv_pairs)


def _attn_sample_kernel(pt_ref, q_ref, kn_ref, wuv_ref, *rest, n_pages_step):
    ckv_refs = rest[:n_pages_step]
    kr_refs = rest[n_pages_step:2 * n_pages_step]
    o_ref, m_ref, l_ref, acc_ref = rest[2 * n_pages_step:]
    del pt_ref
    g = pl.program_id(1)
    n_heads, ds = q_ref.shape[1], q_ref.shape[2]
    rows = n_heads * ds

    @pl.when(g == 0)
    def _():
        m_ref[...] = jnp.full(m_ref.shape, NEG, F32)
        l_ref[...] = jnp.zeros(l_ref.shape, F32)
        acc_ref[...] = jnp.zeros(acc_ref.shape, F32)

    q = q_ref[0].reshape(rows, LATENT_PAD)
    q_lat, q_rope = q[:, :KV_LORA], q[:, KV_LORA:KV_LORA + QK_ROPE]

    def update(state, scores, values):
        m_prev, l_prev, acc = state
        s_max = functools.reduce(jnp.maximum, scores)
        m_new = jnp.maximum(m_prev, jnp.max(s_max, axis=-1, keepdims=True))
        corr = jnp.exp(m_prev - m_new)
        m_wide = jnp.broadcast_to(m_new, scores[0].shape)
        probs = [jnp.exp(s - m_wide) for s in scores]
        p_sum = functools.reduce(jnp.add, probs)
        acc = acc * corr
        for p, val in zip(probs, values):
            acc = acc + _dot(p.astype(BF16), val)
        return m_new, l_prev * corr + jnp.sum(p_sum, axis=-1, keepdims=True), acc

    keys = [c[...].astype(BF16) for c in ckv_refs]
    scores = [_dot_nt(q_lat, kc) + _dot(q_rope, kr[...].astype(BF16)) for kc, kr in zip(keys, kr_refs)]
    state = update((m_ref[...], l_ref[...], acc_ref[...]), scores, keys)
    m_ref[...], l_ref[...], acc_ref[...] = state

    @pl.when(g == pl.num_programs(1) - 1)
    def _():
        kn = kn_ref[0]
        s_new = _dot_nt(q, kn).reshape(n_heads, ds, ds)
        causal = (lax.broadcasted_iota(jnp.int32, (ds, ds), 1) <= lax.broadcasted_iota(jnp.int32, (ds, ds), 0))
        s_new = jnp.where(causal[None], s_new, NEG).reshape(rows, ds)
        _, l_fin, acc_fin = update(state, [s_new], [kn[:, :KV_LORA]])
        o_lat = (acc_fin / l_fin).reshape(n_heads, ds, KV_LORA)

        def store(j, val):
            o_ref[0, :, j * LANES:(j + 1) * LANES] = val.astype(o_ref.dtype)
        _value_up(o_lat, wuv_ref, store)


def attention_sample(qf, kf_new, cache_ckv, cache_krope_t, layer, page_table, w_uv_pairs):
    db, n_heads, ds, _ = qf.shape
    n_pages = page_table.shape[1]
    gp = _pick(n_pages, PAGES_PER_STEP, 1)

    def cache_spec(arr, i):
        return pl.BlockSpec((None, None) + arr.shape[2:], lambda b, g, pt, i=i: (layer, pt[b, g * gp + i], 0, 0))

    grid_spec = pltpu.PrefetchScalarGridSpec(
        num_scalar_prefetch=1,
        grid=(db, n_pages // gp),
        in_specs=[pl.BlockSpec((1, n_heads, ds, LATENT_PAD), lambda b, g, pt: (b, 0, 0, 0)),
                  pl.BlockSpec((1, ds, LATENT_PAD), lambda b, g, pt: (b, 0, 0)),
                  pl.BlockSpec(w_uv_pairs.shape, lambda b, g, pt: (0, 0, 0))]
                 + [cache_spec(cache_ckv, i) for i in range(gp)]
                 + [cache_spec(cache_krope_t, i) for i in range(gp)],
        out_specs=pl.BlockSpec((1, ds, n_heads * V_HEAD), lambda b, g, pt: (b, 0, 0)),
        scratch_shapes=[pltpu.VMEM((n_heads * ds, 1), F32), pltpu.VMEM((n_heads * ds, 1), F32),
                        pltpu.VMEM((n_heads * ds, KV_LORA), F32)],
    )
    return pl.pallas_call(
        functools.partial(_attn_sample_kernel, n_pages_step=gp),
        grid_spec=grid_spec,
        out_shape=jax.ShapeDtypeStruct((db, ds, n_heads * V_HEAD), BF16),
        compiler_params=_params("parallel", "arbitrary"),
        name="attention_sample",
    )(page_table, qf, kf_new, w_uv_pairs, *([cache_ckv] * gp), *([cache_krope_t] * gp))


def _merge_kernel(oa_ref, ob_ref, ga_ref, gb_ref, wa_ref, wb_ref, o_ref):
    ua = _dot(oa_ref[...], wa_ref[...])
    ub = _dot(ob_ref[...], wb_ref[...])
    o_ref[...] = (_sigmoid(ga_ref[...]) * ua + _sigmoid(gb_ref[...]) * ub).astype(o_ref.dtype)


def merge(o_a, o_b, p, ga_off, gb_off, w_up_a, w_up_b):
    n, d_a = o_a.shape
    d = w_up_a.shape[1]
    tm = _pick(n, 1376)
    tn = SEG
    return pl.pallas_call(
        _merge_kernel,
        grid=(n // tm, d // tn),
        in_specs=[pl.BlockSpec((tm, d_a), lambda i, j: (i, 0)),
                  pl.BlockSpec((tm, o_b.shape[1]), lambda i, j: (i, 0)),
                  pl.BlockSpec((tm, tn), lambda i, j: (i, ga_off // tn + j)),
                  pl.BlockSpec((tm, tn), lambda i, j: (i, gb_off // tn + j)),
                  pl.BlockSpec((d_a, tn), lambda i, j: (0, j)),
                  pl.BlockSpec((w_up_b.shape[0], tn), lambda i, j: (0, j))],
        out_specs=pl.BlockSpec((tm, tn), lambda i, j: (i, j)),
        out_shape=jax.ShapeDtypeStruct((n, d), BF16),
        compiler_params=_params("parallel", "arbitrary"),
        name="merge",
    )(o_a, o_b, p, p, w_up_a, w_up_b)


def _wo_kernel(m_ref, x_ref, w_ref, o_ref):
    o_ref[...] = x_ref[...] + _dot(m_ref[...], w_ref[...])


def out_proj(m, x, w_o):
    n, d = x.shape
    tm = _pick(n, 1376)
    tn = SEG
    return pl.pallas_call(
        _wo_kernel,
        grid=(n // tm, d // tn),
        in_specs=[pl.BlockSpec((tm, d), lambda i, j: (i, 0)),
                  pl.BlockSpec((tm, tn), lambda i, j: (i, j)),
                  pl.BlockSpec((d, tn), lambda i, j: (0, j))],
        out_specs=pl.BlockSpec((tm, tn), lambda i, j: (i, j)),
        out_shape=jax.ShapeDtypeStruct((n, d), F32),
        compiler_params=_params("parallel", "arbitrary"),
        name="out_proj",
    )(m, x, w_o)


def _router_kernel(x_ref, g_ref, w_ref, bias_ref, h_ref, r_ref):
    x = x_ref[...]
    h = x * lax.rsqrt(jnp.mean(x * x, -1, keepdims=True) + NORM_EPS) * g_ref[...]
    h_ref[...] = h.astype(h_ref.dtype)
    logits = _dot(h, w_ref[...], precision=HIGHEST)
    biased = logits + bias_ref[...]
    lane_i = lax.broadcasted_iota(jnp.int32, logits.shape, 1)
    lane = lane_i.astype(F32)
    lane_group = (lane_i // EXPERTS_PER_GROUP).astype(F32)
    is_g = (lane_i >= N_EXPERTS) & (lane_i < N_EXPERTS + N_GROUPS)
    big = float(1 << 20)

    def first_argmax(vals):
        mx = jnp.max(vals, axis=-1, keepdims=True)
        return jnp.min(jnp.where(vals == mx, lane, big), axis=-1, keepdims=True)

    def pick(vals, idx):
        return jnp.sum(jnp.where(lane == idx, vals, 0.0), axis=-1, keepdims=True)

    g_lane = first_argmax(jnp.where(is_g, biased, -jnp.inf))
    g_sel = g_lane - float(N_EXPERTS)
    g_max = jnp.max(jnp.where(is_g, logits, -jnp.inf), axis=-1, keepdims=True)
    g_exp = jnp.where(is_g, jnp.exp(logits - g_max), 0.0)
    p_sel = pick(g_exp, g_lane) / jnp.sum(g_exp, axis=-1, keepdims=True)

    in_grp = (lane_i < N_EXPERTS) & (lane_group == g_sel)
    e_biased = jnp.where(in_grp, biased, -jnp.inf)
    i1 = first_argmax(e_biased)
    i2 = first_argmax(jnp.where(lane == i1, -jnp.inf, e_biased))
    l1, l2 = pick(logits, i1), pick(logits, i2)
    mx = jnp.maximum(l1, l2)
    e1, e2 = jnp.exp(l1 - mx), jnp.exp(l2 - mx)
    w1 = e1 / (e1 + e2) * p_sel
    w2 = e2 / (e1 + e2) * p_sel
    r_ref[...] = jnp.where(lane_i == 0, i1, jnp.where(lane_i == 1, i2, jnp.where(lane_i == 2, w1,
                                                                                   jnp.where(lane_i == 3, w2, 0.0))))


def router(x, g, w_router, bias_router):
    n, d = x.shape
    tm = _pick(n, 688)
    return pl.pallas_call(
        _router_kernel,
        grid=(n // tm,),
        in_specs=[pl.BlockSpec((tm, d), lambda i: (i, 0)),
                  pl.BlockSpec((1, d), lambda i: (0, 0)),
                  pl.BlockSpec((d, LANES), lambda i: (0, 0)),
                  pl.BlockSpec((1, LANES), lambda i: (0, 0))],
        out_specs=[pl.BlockSpec((tm, d), lambda i: (i, 0)), pl.BlockSpec((tm, LANES), lambda i: (i, 0))],
        out_shape=[jax.ShapeDtypeStruct((n, d), BF16), jax.ShapeDtypeStruct((n, LANES), F32)],
        compiler_params=_params("parallel"),
        name="router",
    )(x, g, w_router, bias_router)


def _expert_kernel(te_ref, nt_ref, x_ref, gate_ref, wg_ref, wu_ref, wd_ref, o_ref):
    del te_ref

    @pl.when(pl.program_id(0) < nt_ref[0])
    def _():
        x = x_ref[...]
        hg = _dot(x, wg_ref[...].astype(BF16))
        hu = _dot(x, wu_ref[...].astype(BF16))
        act = hg * _sigmoid(hg) * hu * gate_ref[...]
        o_ref[...] = _dot(act.astype(BF16), wd_ref[...].astype(BF16))

    @pl.when(pl.program_id(0) >= nt_ref[0])
    def _():
        o_ref[...] = jnp.zeros(o_ref.shape, o_ref.dtype)


def experts(tile_expert, n_tiles_used, xg, gates, w_gate, w_up, w_down, layer):
    m_pad, d = xg.shape
    ff = w_gate.shape[-1]
    tm = EXPERT_TILE
    grid_spec = pltpu.PrefetchScalarGridSpec(
        num_scalar_prefetch=2,
        grid=(m_pad // tm,),
        in_specs=[pl.BlockSpec((tm, d), lambda i, te, nt: (i, 0)),
                  pl.BlockSpec((tm, 1), lambda i, te, nt: (i, 0)),
                  pl.BlockSpec((None, None, d, ff), lambda i, te, nt: (layer, te[i], 0, 0)),
                  pl.BlockSpec((None, None, d, ff), lambda i, te, nt: (layer, te[i], 0, 0)),
                  pl.BlockSpec((None, None, ff, d), lambda i, te, nt: (layer, te[i], 0, 0))],
        out_specs=pl.BlockSpec((tm, d), lambda i, te, nt: (i, 0)),
    )
    return pl.pallas_call(
        _expert_kernel,
        grid_spec=grid_spec,
        out_shape=jax.ShapeDtypeStruct((m_pad, d), F32),
        compiler_params=_params("arbitrary"),
        name="experts",
    )(tile_expert, n_tiles_used, xg, gates, w_gate, w_up, w_down)


def _combine_kernel(x_ref, e0_ref, e1_ref, g_ref, o_ref):
    x = x_ref[...] + (e0_ref[...] + e1_ref[...])
    o_ref[...] = x * lax.rsqrt(jnp.mean(x * x, -1, keepdims=True) + NORM_EPS) * g_ref[...]


def combine(x, e0, e1, g, row_start):
    n, d = x.shape
    tm = _pick(int(np.gcd(n, row_start)) if row_start else n, 688)
    off = row_start // tm
    row = pl.BlockSpec((tm, d), lambda i: (i, 0))
    shifted = pl.BlockSpec((tm, d), lambda i: (i + off, 0))
    return pl.pallas_call(
        _combine_kernel,
        grid=(n // tm,),
        in_specs=[row, shifted, shifted, pl.BlockSpec((1, d), lambda i: (0, 0))],
        out_specs=row,
        out_shape=jax.ShapeDtypeStruct((n, d), F32),
        compiler_params=_params("parallel"),
        name="combine",
    )(x, e0, e1, g)


def _pad_cols(w, width):
    return jnp.pad(w, ((0, 0), (0, width - w.shape[1])))


def _pad_rows(w, height):
    return jnp.pad(w, ((0, height - w.shape[0]), (0, 0)))


def _swap_halves(w):
    half = w.shape[-1] // 2
    return jnp.concatenate([w[..., half:], w[..., :half]], axis=-1)


def _block_diag_pairs(w):
    h, a, b = w.shape
    z = jnp.zeros((h // 2, a, b), w.dtype)
    top = jnp.concatenate([w[0::2], z], axis=-1)
    bot = jnp.concatenate([z, w[1::2]], axis=-1)
    return jnp.concatenate([top, bot], axis=1)


def _rope_tables(pos):
    half = QK_ROPE // 2
    inv = ROPE_THETA ** (-jnp.arange(half, dtype=F32) / half)
    ang = pos.astype(F32)[:, None] * inv[None, :]
    cos, sin = jnp.cos(ang), jnp.sin(ang)
    cos_t = _pad_cols(jnp.concatenate([cos, cos], -1), LANES)
    sin_t = _pad_cols(jnp.concatenate([-sin, sin], -1), LANES)
    return cos_t, sin_t


def _layer_weights(l, w_in, rwkv_mu, rwkv_w0, rwkv_w2, rwkv_a0, rwkv_a2, rwkv_g2, rwkv_k_k, rwkv_k_a, rwkv_r_k,
                   rwkv_ln_g, rwkv_ln_b, mla_q_norm_g, mla_w_uq, mla_kv_norm_g, mla_w_uk, mla_w_uv, w_up_a,
                   w_up_b, w_o, norm_mix_g):
    d = w_in.shape[1]
    d_a = rwkv_w0.shape[1]
    n_heads = mla_w_uk.shape[2]
    off_kv, off_kr = Q_LORA, Q_LORA + KV_LORA
    off_rw = off_kr + QK_ROPE
    off_ga = off_rw + 3 * d_a + DECAY_LORA + AAA_LORA + GATE_LORA
    off_gb = off_ga + d
    w = w_in[l]
    rw = w[:, off_rw:off_ga]

    def lora_layout(m):
        wl = m[:, 3 * d_a:3 * d_a + DECAY_LORA]
        al = m[:, 3 * d_a + DECAY_LORA:3 * d_a + DECAY_LORA + AAA_LORA]
        gl = m[:, 3 * d_a + DECAY_LORA + AAA_LORA:]
        return jnp.concatenate([_pad_cols(wl, LANES), _pad_cols(al, LANES), gl], axis=1)

    kr = w[:, off_kr:off_rw]
    kv_seg = jnp.concatenate([w[:, off_kv:off_kr], _pad_cols(kr, LANES), _pad_cols(_swap_halves(kr), LANES)], axis=1)
    w_all = jnp.concatenate([rw[:, :3 * d_a], lora_layout(rw), w[:, off_ga:off_gb], w[:, off_gb:],
                             w[:, :Q_LORA], kv_seg], axis=1).astype(BF16)
    mu = rwkv_mu[l][None]
    mu_all = jnp.concatenate([mu[:, :3 * d_a], lora_layout(mu)], axis=1)

    uq = mla_w_uq[l].reshape(Q_LORA, n_heads, QK_NOPE + QK_ROPE)
    w_nope = uq[:, :, :QK_NOPE].reshape(Q_LORA, n_heads * QK_NOPE).astype(BF16)
    rope_cols = uq[:, :, QK_NOPE:]
    pad_heads = lambda m: jnp.pad(m, ((0, 0), (0, 0), (0, LANES - QK_ROPE))).reshape(Q_LORA, n_heads * LANES)
    w_rope = pad_heads(rope_cols).astype(BF16)
    w_rope_sw = pad_heads(_swap_halves(rope_cols)).astype(BF16)
    w_uk_pairs = _block_diag_pairs(jnp.transpose(mla_w_uk[l], (1, 2, 0))).astype(BF16)
    w_uv_pairs = _block_diag_pairs(jnp.transpose(mla_w_uv[l], (1, 0, 2))).astype(BF16)

    return dict(
        w_all=w_all, mu_all=mu_all, norm_mix_g=norm_mix_g[l][None],
        ga_off=3 * d_a + SEG, gb_off=3 * d_a + SEG + d,
        w0=rwkv_w0[l][None], a0=rwkv_a0[l][None], k_k=rwkv_k_k[l][None], k_a=rwkv_k_a[l][None],
        w2=_pad_rows(rwkv_w2[l], LANES).astype(BF16), a2=_pad_rows(rwkv_a2[l], LANES).astype(BF16),
        g2=rwkv_g2[l].astype(BF16),
        r_k=rwkv_r_k[l].reshape(1, d_a), ln_g=rwkv_ln_g[l][None], ln_b=rwkv_ln_b[l][None],
        g_q=mla_q_norm_g[l][None], g_kv=mla_kv_norm_g[l][None],
        w_nope=w_nope, w_rope=w_rope, w_rope_sw=w_rope_sw, w_uk_pairs=w_uk_pairs, w_uv_pairs=w_uv_pairs,
        w_up_a=w_up_a[l].astype(BF16), w_up_b=w_up_b[l].astype(BF16), w_o=w_o[l].astype(BF16),
        n_heads=n_heads, d_a=d_a, lora_layout=lora_layout,
    )


def _token_mixer(x3, pos, shift_prev, wkv_prev, lw, attend):
    b, t, d = x3.shape
    n = b * t
    d_a = lw["d_a"]
    x2 = x3.reshape(n, d)
    p = rms_matmul(x2, lw["norm_mix_g"], lw["w_all"])
    p3 = p.reshape(b, t, p.shape[1])

    chunk = _pick(t, SCAN_CHUNK)
    if t >= 344:
        bb, tt = 1, _pick(t, 344)
        tt_prep = _pick(t, 688, chunk)
    else:
        bb, tt = _pick(b, max(1, 256 // t), 1), t
        tt_prep = t

    def column_tiles(m):
        n_ct = d_a // SEG
        parts = [m[:, part * d_a:(part + 1) * d_a].reshape(-1, n_ct, SEG) for part in range(3)]
        lora = jnp.broadcast_to(m[:, None, 3 * d_a:], (m.shape[0], n_ct, SEG))
        return jnp.concatenate(parts + [lora], axis=-1)[:, :, None, :]

    shift_cols = jnp.concatenate([shift_prev[:, :3 * d_a], lw["lora_layout"](shift_prev)], axis=1)
    rt, qu, kh, bh, kt, bt, v, gam, bonus, g = rwkv_prepare(
        p3, column_tiles(shift_cols), column_tiles(lw["mu_all"])[0], lw["w0"], lw["a0"], lw["k_k"], lw["k_a"],
        lw["r_k"], lw["w2"], lw["a2"], lw["g2"], bb, tt_prep, chunk)
    y, s_new = rwkv_scan(rt, qu, kh, bh, kt, bt, v, gam, wkv_prev, chunk)
    flat = lambda a: a.reshape(n, d_a)
    o_a = rwkv_out(flat(y), flat(bonus), flat(g), lw["ln_g"], lw["ln_b"])

    cos_t, sin_t = _rope_tables(pos)
    qf, kf, ckv, krope = mla_project(p3, cos_t, sin_t, lw["g_q"], lw["g_kv"], lw["w_nope"], lw["w_rope"],
                                     lw["w_rope_sw"], lw["w_uk_pairs"], lw["n_heads"], bb, tt)
    o_b = attend(qf, kf).reshape(n, -1)

    m = merge(o_a, o_b, p, lw["ga_off"], lw["gb_off"], lw["w_up_a"], lw["w_up_b"])
    x_new = out_proj(m, x2, lw["w_o"])

    last = p3[:, -1, :]
    shift_new = jnp.concatenate([last[:, :3 * d_a],
                                 last[:, 3 * d_a:3 * d_a + DECAY_LORA],
                                 last[:, 3 * d_a + LANES:3 * d_a + LANES + AAA_LORA],
                                 last[:, 3 * d_a + 2 * LANES:3 * d_a + SEG]], axis=1)
    return x_new, (ckv, krope, s_new, shift_new)


def _moe(x_rows, norm_g, router_group_w, router_group_b, router_expert_w, router_expert_b,
         w_gate, w_up, w_down, layer, final_g):
    w_router = _pad_cols(jnp.concatenate([router_expert_w, router_group_w], axis=1), LANES)
    b_router = _pad_cols(jnp.concatenate([router_expert_b, router_group_b])[None], LANES)
    hs, routes = zip(*[router(x, norm_g, w_router, b_router) for x in x_rows])
    h = jnp.concatenate(hs, axis=0)
    route = jnp.concatenate(routes, axis=0)
    n = h.shape[0]
    eid = route[:, :2].astype(jnp.int32).reshape(-1)
    gate = route[:, 2:4].reshape(-1)

    tm = EXPERT_TILE
    n_tiles = -(-(2 * n + N_EXPERTS * (tm - 1)) // tm)
    m_pad = n_tiles * tm
    experts_iota = jnp.arange(N_EXPERTS, dtype=jnp.int32)
    order = jnp.argsort(eid, stable=True).astype(jnp.int32)
    rank = jnp.argsort(order).astype(jnp.int32)
    counts = jnp.sum((eid[:, None] == experts_iota[None, :]).astype(jnp.int32), axis=0)
    padded = ((counts + tm - 1) // tm) * tm
    seg_end = jnp.cumsum(padded)
    seg_start = seg_end - padded
    cnt_start = jnp.cumsum(counts) - counts
    dest = jnp.take(seg_start - cnt_start, eid, mode="clip") + rank
    tile_start = jnp.arange(n_tiles, dtype=jnp.int32) * tm
    tile_expert = jnp.minimum(jnp.sum((seg_end[None, :] <= tile_start[:, None]).astype(jnp.int32), axis=1),
                              N_EXPERTS - 1)
    n_used = (seg_end[-1] // tm).astype(jnp.int32).reshape(1)
    row_expert = jnp.repeat(tile_expert, tm)
    within = jnp.arange(m_pad, dtype=jnp.int32) - jnp.take(seg_start, row_expert, mode="clip")
    valid = within < jnp.take(counts, row_expert, mode="clip")
    row_assign = jnp.take(order, jnp.take(cnt_start, row_expert, mode="clip") + within, mode="clip")
    row_token = jnp.where(valid, row_assign // 2, 0)
    row_gate = jnp.where(valid, jnp.take(gate, row_assign, mode="clip"), 0.0)

    xg = jnp.take(h, row_token, axis=0, mode="clip")
    eo = experts(tile_expert, n_used, xg, row_gate[:, None], w_gate, w_up, w_down, layer)
    dest2 = dest.reshape(n, 2)
    e0 = jnp.take(eo, dest2[:, 0], axis=0, mode="clip")
    e1 = jnp.take(eo, dest2[:, 1], axis=0, mode="clip")

    outs, start = [], 0
    for x in x_rows:
        outs.append(combine(x, e0, e1, final_g, start))
        start += x.shape[0]
    return outs


def kernel(x_prompt, x_sample, cache_ckv, cache_krope, state_wkv, state_shift, page_table, meta_tokens, norm_mix_g, w_in, rwkv_mu, rwkv_w0, rwkv_w2, rwkv_a0, rwkv_a2, rwkv_g2, rwkv_k_k, rwkv_k_a, rwkv_r_k, rwkv_ln_g, rwkv_ln_b, mla_q_norm_g, mla_w_uq, mla_kv_norm_g, mla_w_uk, mla_w_uv, w_up_a, w_up_b, w_o, norm_ffn_g, router_group_w, router_group_b, router_expert_w, router_expert_b, expert_w_gate, expert_w_up, expert_w_down, norm_final_g):
    depth = w_in.shape[0]
    assert depth == 1, "the final norm is fused into the MoE combine, which assumes a single layer"
    b, s, d = x_prompt.shape
    db, ds, _ = x_sample.shape
    past = page_table.shape[1] * PAGE_SIZE
    t = N_META + s
    n_heads_a = state_wkv.shape[2]
    rwkv_cols = state_shift.shape[-1]

    xp = jnp.concatenate([jnp.broadcast_to(meta_tokens[None], (b, N_META, d)), x_prompt], axis=1)
    l = 0
    lw = _layer_weights(l, w_in, rwkv_mu, rwkv_w0, rwkv_w2, rwkv_a0, rwkv_a2, rwkv_g2, rwkv_k_k, rwkv_k_a,
                        rwkv_r_k, rwkv_ln_g, rwkv_ln_b, mla_q_norm_g, mla_w_uq, mla_kv_norm_g, mla_w_uk,
                        mla_w_uv, w_up_a, w_up_b, w_o, norm_mix_g)

    xp1, st_p = _token_mixer(xp, jnp.arange(t), jnp.zeros((b, rwkv_cols), F32),
                             jnp.zeros((b, n_heads_a, HEAD_A, HEAD_A), F32), lw,
                             lambda qf, kf: attention_prompt(qf, kf, lw["w_uv_pairs"]))
    xs1, st_s = _token_mixer(x_sample, past + jnp.arange(ds), state_shift[l], state_wkv[l], lw,
                             lambda qf, kf: attention_sample(qf, kf, cache_ckv, jnp.swapaxes(cache_krope, 2, 3), l,
                                                             page_table, lw["w_uv_pairs"]))

    yp, ys = _moe([xp1, xs1], norm_ffn_g[l][None], router_group_w[l], router_group_b[l], router_expert_w[l],
                  router_expert_b[l], expert_w_gate, expert_w_up, expert_w_down, l, norm_final_g[None])
    y_prompt = yp.reshape(b, t, d)[:, N_META:]
    y_sample = ys.reshape(db, ds, d)
    return (y_prompt, y_sample,
            st_p[0][None], st_p[1][None], st_p[2][None], st_p[3][None],
            st_s[0][None], st_s[1][None], st_s[2][None], st_s[3][None])
```

```python
import functools

import jax
import jax.numpy as jnp
import numpy as np
from jax import lax
from jax.experimental import pallas as pl
from jax.experimental.pallas import tpu as pltpu

F32 = jnp.float32
BF16 = jnp.bfloat16
HIGHEST = lax.Precision.HIGHEST

N_META = 16
NORM_EPS = 1e-6
NEG = -1e30
HEAD_A = 64
DECAY_LORA = 96
AAA_LORA = 96
GATE_LORA = 256
LNX_EPS = 64e-5
QK_NOPE = 64
QK_ROPE = 32
V_HEAD = 64
Q_LORA = 512
KV_LORA = 256
ROPE_THETA = 10000.0
PAGE_SIZE = 128
N_GROUPS = 4
EXPERTS_PER_GROUP = 8
N_EXPERTS = N_GROUPS * EXPERTS_PER_GROUP

LANES = 128
HEAD_PAIR = 2 * HEAD_A
SEG = 512
LATENT_PAD = KV_LORA + LANES
VMEM_LIMIT = 56 * 1024 * 1024
SCAN_CHUNK = 16
SCAN_BATCH = 4
EXPERT_TILE = 256
PAGES_PER_STEP = 32
SAMPLE_SUBGROUP = 8
ATTN_HEAD_GROUP = 8


def _pick(n, target, mult=8):
    best = None
    for d in range(mult, min(n, target) + 1, mult):
        if n % d == 0:
            best = d
    return best if best is not None else n


def _params(*sem):
    return pltpu.CompilerParams(dimension_semantics=sem, vmem_limit_bytes=VMEM_LIMIT)


def _dot(a, b, **kw):
    return jnp.dot(a, b, preferred_element_type=F32, **kw)


def _dot_nt(a, b):
    return lax.dot_general(a, b, (((1,), (1,)), ((), ())), preferred_element_type=F32)


def _dot_tn(a, b):
    return lax.dot_general(a, b, (((0,), (0,)), ((), ())), preferred_element_type=F32)


def _sigmoid(x):
    return 1.0 / (1.0 + jnp.exp(-x))


def _head_ones(scale=1.0):
    r = lax.broadcasted_iota(jnp.int32, (LANES, LANES), 0) // HEAD_A
    c = lax.broadcasted_iota(jnp.int32, (LANES, LANES), 1) // HEAD_A
    return jnp.where(r == c, scale, 0.0).astype(F32)


def _rms_mm_kernel(x_ref, g_ref, w_ref, o_ref, h_ref):
    @pl.when(pl.program_id(1) == 0)
    def _():
        x = x_ref[...]
        ms = jnp.mean(x * x, axis=-1, keepdims=True)
        h_ref[...] = (x * lax.rsqrt(ms + NORM_EPS) * g_ref[...]).astype(BF16)

    o_ref[...] = _dot(h_ref[...], w_ref[...])


def rms_matmul(x, g, w):
    n, k = x.shape
    m = w.shape[1]
    tm = _pick(n, 1376)
    tn = SEG
    return pl.pallas_call(
        _rms_mm_kernel,
        grid=(n // tm, m // tn),
        in_specs=[pl.BlockSpec((tm, k), lambda i, j: (i, 0)),
                  pl.BlockSpec((1, k), lambda i, j: (0, 0)),
                  pl.BlockSpec((k, tn), lambda i, j: (0, j))],
        out_specs=pl.BlockSpec((tm, tn), lambda i, j: (i, j)),
        out_shape=jax.ShapeDtypeStruct((n, m), F32),
        scratch_shapes=[pltpu.VMEM((tm, k), BF16)],
        compiler_params=_params("parallel", "arbitrary"),
        name="rms_matmul",
    )(x, g, w)


def _prepare_kernel(pr_ref, pk_ref, pv_ref, pl_ref, sh_ref, mu_ref, w0_ref, a0_ref, kk_ref, ka_ref, rk_ref,
                    w2_ref, a2_ref, g2_ref,
                    rt_out, qu_out, kh_out, bh_out, kt_out, bt_out, v_out, gam_out, bonus_out, g_out,
                    carry_ref, *, chunk):
    tc = pr_ref.shape[-1]
    ct = pl.program_id(2)

    @pl.when(pl.program_id(1) == 0)
    def _():
        carry_ref[ct] = sh_ref[...]

    def shifted(ref, lo, width):
        x = ref[...]
        prev_row = carry_ref[ct, :, :, lo:lo + width]
        t_idx = lax.broadcasted_iota(jnp.int32, x.shape, 1)
        prev = jnp.where(t_idx == 0, prev_row, pltpu.roll(x, 1, axis=1))
        carry_ref[ct, :, :, lo:lo + width] = x[:, x.shape[1] - 1:, :]
        xs = x + (prev - x) * mu_ref[:, lo:lo + width]
        return xs.reshape(x.shape[0] * x.shape[1], width)

    bb, tt, _ = pr_ref.shape
    rows = bb * tt
    r = shifted(pr_ref, 0, tc)
    k = shifted(pk_ref, tc, tc)
    v = shifted(pv_ref, 2 * tc, tc)
    lora = shifted(pl_ref, 3 * tc, SEG)
    wl, al, gl = lora[:, :LANES], lora[:, LANES:2 * LANES], lora[:, 2 * LANES:]

    z = -(w0_ref[...] + _dot(jnp.tanh(wl).astype(BF16), w2_ref[...]))
    softplus = jnp.maximum(z, 0.0) + jnp.log(1.0 + jnp.exp(-jnp.abs(z)))
    lw = -jnp.exp(-softplus - 0.5)
    a = _sigmoid(a0_ref[...] + _dot(al.astype(BF16), a2_ref[...]))
    g = _dot(_sigmoid(gl).astype(BF16), g2_ref[...])

    kkf = k * kk_ref[...]
    ones = _head_ones()
    per_head = lambda x: jnp.concatenate(
        [_dot(x[:, j * LANES:(j + 1) * LANES], ones, precision=HIGHEST) for j in range(tc // LANES)], axis=-1)
    kk = kkf * lax.rsqrt(jnp.maximum(per_head(jnp.square(kkf)), 1e-24))
    k_mod = k * (1.0 + (a - 1.0) * ka_ref[...])
    b = kk * a
    bonus = per_head(r * k_mod * rk_ref[...]) * v

    pos = lax.broadcasted_iota(jnp.int32, (rows, tc), 0) % chunk
    cum = lw
    sh = 1
    while sh < chunk:
        cum = cum + jnp.where(pos >= sh, pltpu.roll(cum, sh, axis=0), 0.0)
        sh *= 2
    total = jnp.where(pos == chunk - 1, cum, 0.0)
    sh = 1
    while sh < chunk:
        total = total + pltpu.roll(total, rows - sh, axis=0)
        sh *= 2
    e_neg = jnp.exp(-cum)
    e_tail = jnp.exp(total - cum)

    def st(ref, val):
        ref[...] = val.reshape(bb, tt, tc).astype(ref.dtype)

    st(rt_out, r * jnp.exp(cum))
    st(qu_out, kk * jnp.exp(cum - lw))
    st(kh_out, k_mod * e_neg)
    st(bh_out, b * e_neg)
    st(kt_out, k_mod * e_tail)
    st(bt_out, b * e_tail)
    st(v_out, v)
    st(bonus_out, bonus)
    st(g_out, g)
    gam_out[...] = jnp.exp(total).reshape(bb, tt // chunk, chunk, tc)[:, :, 0:1, :]


def rwkv_prepare(p3, shift4, mu3, w0, a0, k_k, k_a, r_k, w2, a2, g2, bb, tt, chunk):
    b, t, _ = p3.shape
    d_a = w0.shape[-1]
    tc = SEG
    n_ct = d_a // tc
    width = 3 * tc + SEG
    row = lambda part: pl.BlockSpec((bb, tt, tc), lambda i, j, c, part=part: (i, j, part * n_ct + c))
    vec = pl.BlockSpec((1, tc), lambda i, j, c: (0, c))
    mat = lambda a: pl.BlockSpec((a.shape[0], tc), lambda i, j, c: (0, c))
    out_spec = pl.BlockSpec((bb, tt, tc), lambda i, j, c: (i, j, c))
    seq = lambda dt: jax.ShapeDtypeStruct((b, t, d_a), dt)
    return pl.pallas_call(
        functools.partial(_prepare_kernel, chunk=chunk),
        grid=(b // bb, t // tt, n_ct),
        in_specs=[row(0), row(1), row(2),
                  pl.BlockSpec((bb, tt, SEG), lambda i, j, c: (i, j, 3 * d_a // SEG)),
                  pl.BlockSpec((bb, None, 1, width), lambda i, j, c: (i, c, 0, 0)),
                  pl.BlockSpec((None, 1, width), lambda i, j, c: (c, 0, 0)),
                  vec, vec, vec, vec, vec, mat(w2), mat(a2), mat(g2)],
        out_specs=[out_spec] * 7
                  + [pl.BlockSpec((bb, tt // chunk, 1, tc), lambda i, j, c: (i, j, 0, c)), out_spec, out_spec],
        out_shape=[seq(BF16)] * 7 + [jax.ShapeDtypeStruct((b, t // chunk, 1, d_a), F32), seq(F32), seq(F32)],
        scratch_shapes=[pltpu.VMEM((n_ct, bb, 1, width), F32)],
        compiler_params=_params("parallel", "arbitrary", "arbitrary"),
        name="rwkv_prepare",
    )(p3, p3, p3, p3, shift4, mu3, w0, a0, k_k, k_a, r_k, w2, a2, g2)


def _scan_kernel(rt_ref, qu_ref, kh_ref, bh_ref, kt_ref, bt_ref, v_ref, gam_ref, s0_ref, y_ref, sf_ref, st_ref,
                 *, chunk):
    c_idx = pl.program_id(1)
    nb, n_pairs = st_ref.shape[:2]
    chains = [(bi, j) for bi in range(nb) for j in range(n_pairs)]
    c2x = 2 * chunk

    @pl.when(c_idx == 0)
    def _():
        zero = jnp.zeros((HEAD_A, HEAD_A), F32)
        for bi, j in chains:
            top = jnp.concatenate([s0_ref[bi, 2 * j], zero], axis=1)
            bot = jnp.concatenate([zero, s0_ref[bi, 2 * j + 1]], axis=1)
            st_ref[bi, j] = jnp.concatenate([top, bot], axis=0)

    r2 = lax.broadcasted_iota(jnp.int32, (c2x, c2x), 0)
    c2 = lax.broadcasted_iota(jnp.int32, (c2x, c2x), 1)
    strict = c2 < r2
    incl = c2 <= r2
    head0 = lax.broadcasted_iota(jnp.int32, (chunk, LANES), 1) < HEAD_A

    def stack2(ref, bi, j):
        x = ref[bi, :, j * LANES:(j + 1) * LANES].astype(F32)
        return jnp.concatenate([jnp.where(head0, x, 0.0), jnp.where(head0, 0.0, x)], axis=0).astype(BF16)

    each = lambda f, *lists: [f(*args) for args in zip(*lists)]
    s_prev = [st_ref[bi, j] for bi, j in chains]
    gam = [gam_ref[bi, 0, :, j * LANES:(j + 1) * LANES] for bi, j in chains]
    q_u, q_r, k_h, b_h, k_t, b_t, v2 = [[stack2(ref, bi, j) for bi, j in chains]
                                        for ref in (qu_ref, rt_ref, kh_ref, bh_ref, kt_ref, bt_ref, v_ref)]
    qq = each(lambda a, b: jnp.concatenate([a, b], axis=0), q_u, q_r)
    s_bf = [s.astype(BF16) for s in s_prev]
    sk = each(_dot_nt, qq, k_h)
    sb = each(_dot_nt, qq, b_h)
    ss = each(_dot_nt, qq, s_bf)
    a_vk = [jnp.where(strict, m[:c2x], 0.0).astype(BF16) for m in sk]
    power = [jnp.where(strict, -m[:c2x], 0.0) for m in sb]
    a_rk = [jnp.where(incl, m[c2x:], 0.0).astype(BF16) for m in sk]
    a_rb = [jnp.where(incl, m[c2x:], 0.0).astype(BF16) for m in sb]
    x = each(lambda s, a, v: s[:c2x] + _dot(a, v), ss, a_vk, v2)
    steps = max(1, int(np.ceil(np.log2(chunk))))
    for s in range(steps):
        pb = [p.astype(BF16) for p in power]
        x = each(lambda xi, p: xi + _dot(p, xi.astype(BF16)), x, pb)
        if s + 1 < steps:
            power = each(_dot, pb, pb)
    u2 = [xi.astype(BF16) for xi in x]
    y2 = each(lambda s, ak, v, ab, u: s[c2x:] + _dot(ak, v) - _dot(ab, u), ss, a_rk, v2, a_rb, u2)
    s_new = each(lambda s, g, v, u, kt, bt: s * g + _dot_tn(jnp.concatenate([v, -u], axis=0),
                                                              jnp.concatenate([kt, bt], axis=0)),
                 s_prev, gam, v2, u2, k_t, b_t)

    for (bi, j), y, s in zip(chains, y2, s_new):
        y_ref[bi, :, j * LANES:(j + 1) * LANES] = y[:chunk] + y[chunk:]
        st_ref[bi, j] = s

    @pl.when(c_idx == pl.num_programs(1) - 1)
    def _():
        for (bi, j), s in zip(chains, s_new):
            sf_ref[bi, 2 * j] = s[:HEAD_A, :HEAD_A]
            sf_ref[bi, 2 * j + 1] = s[HEAD_A:, HEAD_A:]


def rwkv_scan(rt, qu, kh, bh, kt, bt, v, gam, s0, chunk):
    bsz, t, d_a = rt.shape
    n_pairs = d_a // LANES
    nb = SCAN_BATCH if bsz % SCAN_BATCH == 0 else 1
    seq = pl.BlockSpec((nb, chunk, d_a), lambda i, c: (i, c, 0))
    st = pl.BlockSpec((nb, 2 * n_pairs, HEAD_A, HEAD_A), lambda i, c: (i, 0, 0, 0))
    return pl.pallas_call(
        functools.partial(_scan_kernel, chunk=chunk),
        grid=(bsz // nb, t // chunk),
        in_specs=[seq] * 7 + [pl.BlockSpec((nb, 1, 1, d_a), lambda i, c: (i, c, 0, 0)), st],
        out_specs=[seq, st],
        out_shape=[jax.ShapeDtypeStruct((bsz, t, d_a), F32),
                   jax.ShapeDtypeStruct((bsz, 2 * n_pairs, HEAD_A, HEAD_A), F32)],
        scratch_shapes=[pltpu.VMEM((nb, n_pairs, LANES, LANES), F32)],
        compiler_params=_params("parallel", "arbitrary"),
        name="rwkv_scan",
    )(rt, qu, kh, bh, kt, bt, v, gam, s0)


def _rwkv_out_kernel(y_ref, bonus_ref, g_ref, lng_ref, lnb_ref, o_ref):
    mean_m = _head_ones(1.0 / HEAD_A)
    for j in range(y_ref.shape[-1] // LANES):
        sl = slice(j * LANES, (j + 1) * LANES)
        y = y_ref[:, sl]
        d = y - _dot(y, mean_m, precision=HIGHEST)
        var = _dot(d * d, mean_m, precision=HIGHEST)
        yn = d * lax.rsqrt(var + LNX_EPS) * lng_ref[:, sl] + lnb_ref[:, sl]
        o_ref[:, sl] = ((yn + bonus_ref[:, sl]) * g_ref[:, sl]).astype(o_ref.dtype)


def rwkv_out(y, bonus, g, ln_g, ln_b):
    n, d_a = y.shape
    tm = _pick(n, 688)
    row = pl.BlockSpec((tm, d_a), lambda i: (i, 0))
    vec = pl.BlockSpec((1, d_a), lambda i: (0, 0))
    return pl.pallas_call(
        _rwkv_out_kernel,
        grid=(n // tm,),
        in_specs=[row] * 3 + [vec] * 2,
        out_specs=row,
        out_shape=jax.ShapeDtypeStruct((n, d_a), BF16),
        compiler_params=_params("parallel"),
        name="rwkv_out",
    )(y, bonus, g, ln_g, ln_b)


def _mla_kernel(pq_ref, pkv_ref, cos_ref, sin_ref, gq_ref, gkv_ref, wn_ref, wr_ref, wrs_ref, wuk_ref,
                qf_ref, kf_ref, ckv_ref, kr_ref, *, scale):
    bb, tt, _ = pq_ref.shape
    rows = bb * tt
    n_heads = qf_ref.shape[1]
    cos = jnp.broadcast_to(cos_ref[...][None], (bb, tt, LANES)).reshape(rows, LANES)
    sin = jnp.broadcast_to(sin_ref[...][None], (bb, tt, LANES)).reshape(rows, LANES)

    pq = pq_ref[...].reshape(rows, Q_LORA)
    c_q = (pq * lax.rsqrt(jnp.mean(pq * pq, -1, keepdims=True) + NORM_EPS) * gq_ref[...]).astype(BF16)
    q_nope = _dot(c_q, wn_ref[...]).astype(BF16)
    q_rope = _dot(c_q, wr_ref[...])
    q_rope_sw = _dot(c_q, wrs_ref[...])
    for j in range(n_heads // 2):
        lat = _dot(q_nope[:, j * LANES:(j + 1) * LANES], wuk_ref[j]) * scale
        for hh in range(2):
            h = 2 * j + hh
            qf_ref[:, h, :, :KV_LORA] = lat[:, hh * KV_LORA:(hh + 1) * KV_LORA].reshape(bb, tt, KV_LORA).astype(BF16)
            sl = slice(h * LANES, (h + 1) * LANES)
            roped = (q_rope[:, sl] * cos + q_rope_sw[:, sl] * sin) * scale
            qf_ref[:, h, :, KV_LORA:] = roped.reshape(bb, tt, LANES).astype(BF16)

    pkv = pkv_ref[...].reshape(rows, SEG)
    kv = pkv[:, :KV_LORA]
    ckv = kv * lax.rsqrt(jnp.mean(kv * kv, -1, keepdims=True) + NORM_EPS) * gkv_ref[...]
    krope = pkv[:, KV_LORA:KV_LORA + LANES] * cos + pkv[:, KV_LORA + LANES:] * sin
    ckv_ref[...] = ckv.reshape(bb, tt, KV_LORA)
    kr_ref[...] = krope[:, :QK_ROPE].reshape(bb, tt, QK_ROPE)
    kf_ref[:, :, :KV_LORA] = ckv.reshape(bb, tt, KV_LORA).astype(BF16)
    kf_ref[:, :, KV_LORA:] = krope.reshape(bb, tt, LANES).astype(BF16)


def mla_project(p3, cos_t, sin_t, g_q, g_kv, w_nope, w_rope, w_rope_sw, w_uk_pairs, n_heads, bb, tt):
    b, t, width = p3.shape
    q_blk = (width - 2 * SEG) // SEG
    full = lambda a: pl.BlockSpec(a.shape, lambda i, j: (0,) * a.ndim)
    scale = float((QK_NOPE + QK_ROPE) ** -0.5)
    return pl.pallas_call(
        functools.partial(_mla_kernel, scale=scale),
        grid=(b // bb, t // tt),
        in_specs=[pl.BlockSpec((bb, tt, SEG), lambda i, j: (i, j, q_blk)),
                  pl.BlockSpec((bb, tt, SEG), lambda i, j: (i, j, q_blk + 1)),
                  pl.BlockSpec((tt, LANES), lambda i, j: (j, 0)),
                  pl.BlockSpec((tt, LANES), lambda i, j: (j, 0)),
                  full(g_q), full(g_kv), full(w_nope), full(w_rope), full(w_rope_sw), full(w_uk_pairs)],
        out_specs=[pl.BlockSpec((bb, n_heads, tt, LATENT_PAD), lambda i, j: (i, 0, j, 0)),
                   pl.BlockSpec((bb, tt, LATENT_PAD), lambda i, j: (i, j, 0)),
                   pl.BlockSpec((bb, tt, KV_LORA), lambda i, j: (i, j, 0)),
                   pl.BlockSpec((bb, tt, QK_ROPE), lambda i, j: (i, j, 0))],
        out_shape=[jax.ShapeDtypeStruct((b, n_heads, t, LATENT_PAD), BF16),
                   jax.ShapeDtypeStruct((b, t, LATENT_PAD), BF16),
                   jax.ShapeDtypeStruct((b, t, KV_LORA), F32),
                   jax.ShapeDtypeStruct((b, t, QK_ROPE), F32)],
        compiler_params=_params("parallel", "parallel"),
        name="mla_project",
    )(p3, p3, cos_t, sin_t, g_q, g_kv, w_nope, w_rope, w_rope_sw, w_uk_pairs)


def _value_up(o_lat, wuv_ref, store):
    n_heads = o_lat.shape[0]
    for j in range(n_heads // 2):
        lhs = jnp.concatenate([o_lat[2 * j], o_lat[2 * j + 1]], axis=-1).astype(BF16)
        store(j, _dot(lhs, wuv_ref[j]))


def _attn_prompt_kernel(q_ref, k_ref, wuv_ref, o_ref, m_ref, l_ref, acc_ref, *, tq):
    qi, ki = pl.program_id(1), pl.program_id(2)
    n_heads = q_ref.shape[1]
    rows = n_heads * tq

    @pl.when(ki == 0)
    def _():
        m_ref[...] = jnp.full(m_ref.shape, NEG, F32)
        l_ref[...] = jnp.zeros(l_ref.shape, F32)
        acc_ref[...] = jnp.zeros(acc_ref.shape, F32)

    def step(diagonal):
        k = k_ref[0]
        hg = ATTN_HEAD_GROUP if n_heads % ATTN_HEAD_GROUP == 0 else n_heads
        grows = hg * tq

        def scores(g):
            return _dot_nt(q_ref[0, g * hg:(g + 1) * hg].reshape(grows, LATENT_PAD), k)

        def softmax_pv(g, s):
            sl = slice(g * grows, (g + 1) * grows)
            if diagonal:
                causal = (lax.broadcasted_iota(jnp.int32, (tq, tq), 1)
                          <= lax.broadcasted_iota(jnp.int32, (tq, tq), 0))
                s = jnp.where(causal[None], s.reshape(hg, tq, tq), NEG).reshape(grows, tq)
            m_prev = m_ref[sl]
            m_new = jnp.maximum(m_prev, jnp.max(s, axis=-1, keepdims=True))
            corr = jnp.exp(m_prev - m_new)
            p = jnp.exp(s - m_new)
            l_ref[sl] = l_ref[sl] * corr + jnp.sum(p, axis=-1, keepdims=True)
            acc_ref[sl] = acc_ref[sl] * corr + _dot(p.astype(BF16), k[:, :KV_LORA])
            m_ref[sl] = m_new

        n_groups = n_heads // hg
        s_next = scores(0)
        for g in range(n_groups):
            s_cur = s_next
            if g + 1 < n_groups:
                s_next = scores(g + 1)
            softmax_pv(g, s_cur)

    @pl.when(ki < qi)
    def _():
        step(False)

    @pl.when(ki == qi)
    def _():
        step(True)
        o_lat = (acc_ref[...] / l_ref[...]).reshape(n_heads, tq, KV_LORA)

        def store(j, val):
            o_ref[0, :, j * LANES:(j + 1) * LANES] = val.astype(o_ref.dtype)
        _value_up(o_lat, wuv_ref, store)


def attention_prompt(qf, kf, w_uv_pairs):
    b, n_heads, t, _ = qf.shape
    tq = _pick(t, 344)
    nq = t // tq
    return pl.pallas_call(
        functools.partial(_attn_prompt_kernel, tq=tq),
        grid=(b, nq, nq),
        in_specs=[pl.BlockSpec((1, n_heads, tq, LATENT_PAD), lambda i, q, k: (i, 0, q, 0)),
                  pl.BlockSpec((1, tq, LATENT_PAD), lambda i, q, k: (i, jnp.minimum(k, q), 0)),
                  pl.BlockSpec(w_uv_pairs.shape, lambda i, q, k: (0, 0, 0))],
        out_specs=pl.BlockSpec((1, tq, n_heads * V_HEAD), lambda i, q, k: (i, q, 0)),
        out_shape=jax.ShapeDtypeStruct((b, t, n_heads * V_HEAD), BF16),
        scratch_shapes=[pltpu.VMEM((n_heads * tq, 1), F32), pltpu.VMEM((n_heads * tq, 1), F32),
                        pltpu.VMEM((n_heads * tq, KV_LORA), F32)],
        compiler_params=_params("parallel", "parallel", "arbitrary"),
        name="attention_prompt",
    )(qf, kf, w_uv_pairs)


def _attn_sample_kernel(pt_ref, q_ref, kn_ref, wuv_ref, ckv_hbm, kr_hbm, o_ref, kbuf, rbuf, sem, m_ref, l_ref,
                        acc_ref, *, n_pages_step, layer):
    b, g = pl.program_id(0), pl.program_id(1)
    n_groups = pl.num_programs(1)
    step = b * n_groups + g
    slot = step % 2
    n_heads, ds = q_ref.shape[1], q_ref.shape[2]
    rows = n_heads * ds

    def page_copies(bb, gg, sl):
        copies = []
        for i in range(n_pages_step):
            page = pt_ref[bb, gg * n_pages_step + i]
            copies.append(pltpu.make_async_copy(ckv_hbm.at[layer, page], kbuf.at[sl, i], sem.at[sl]))
            copies.append(pltpu.make_async_copy(kr_hbm.at[layer, page], rbuf.at[sl, i], sem.at[sl]))
        return copies

    @pl.when(step == 0)
    def _():
        for c in page_copies(b, g, slot):
            c.start()

    @pl.when(step + 1 < pl.num_programs(0) * n_groups)
    def _():
        wrap = g + 1 == n_groups
        for c in page_copies(jnp.where(wrap, b + 1, b), jnp.where(wrap, 0, g + 1), 1 - slot):
            c.start()

    for c in page_copies(b, g, slot):
        c.wait()
    ckv_refs = [kbuf.at[slot, i] for i in range(n_pages_step)]
    kr_refs = [rbuf.at[slot, i] for i in range(n_pages_step)]

    @pl.when(g == 0)
    def _():
        m_ref[...] = jnp.full(m_ref.shape, NEG, F32)
        l_ref[...] = jnp.zeros(l_ref.shape, F32)
        acc_ref[...] = jnp.zeros(acc_ref.shape, F32)

    q = q_ref[0].reshape(rows, LATENT_PAD)
    q_lat, q_rope = q[:, :KV_LORA], q[:, KV_LORA:KV_LORA + QK_ROPE]

    def update(state, scores, values):
        m_prev, l_prev, acc = state
        s_max = functools.reduce(jnp.maximum, scores)
        m_new = jnp.maximum(m_prev, jnp.max(s_max, axis=-1, keepdims=True))
        corr = jnp.exp(m_prev - m_new)
        m_wide = jnp.broadcast_to(m_new, scores[0].shape)
        probs = [jnp.exp(s - m_wide) for s in scores]
        p_sum = functools.reduce(jnp.add, probs)
        acc = acc * corr
        for p, val in zip(probs, values):
            acc = acc + _dot(p.astype(BF16), val)
        return m_new, l_prev * corr + jnp.sum(p_sum, axis=-1, keepdims=True), acc

    def scores_of(lo, hi):
        keys = [c[...].astype(BF16) for c in ckv_refs[lo:hi]]
        return keys, [_dot_nt(q_lat, kc) + _dot(q_rope, kr[...].astype(BF16))
                      for kc, kr in zip(keys, kr_refs[lo:hi])]

    sub = SAMPLE_SUBGROUP if n_pages_step % SAMPLE_SUBGROUP == 0 else n_pages_step
    state = (m_ref[...], l_ref[...], acc_ref[...])
    ahead = scores_of(0, sub)
    for lo in range(0, n_pages_step, sub):
        keys, scores = ahead
        if lo + sub < n_pages_step:
            ahead = scores_of(lo + sub, lo + 2 * sub)
        state = update(state, scores, keys)
    m_ref[...], l_ref[...], acc_ref[...] = state

    @pl.when(g == pl.num_programs(1) - 1)
    def _():
        kn = kn_ref[0]
        s_new = _dot_nt(q, kn).reshape(n_heads, ds, ds)
        causal = (lax.broadcasted_iota(jnp.int32, (ds, ds), 1) <= lax.broadcasted_iota(jnp.int32, (ds, ds), 0))
        s_new = jnp.where(causal[None], s_new, NEG).reshape(rows, ds)
        _, l_fin, acc_fin = update(state, [s_new], [kn[:, :KV_LORA]])
        o_lat = (acc_fin / l_fin).reshape(n_heads, ds, KV_LORA)

        def store(j, val):
            o_ref[0, :, j * LANES:(j + 1) * LANES] = val.astype(o_ref.dtype)
        _value_up(o_lat, wuv_ref, store)


def attention_sample(qf, kf_new, cache_ckv, cache_krope_t, layer, page_table, w_uv_pairs):
    db, n_heads, ds, _ = qf.shape
    n_pages = page_table.shape[1]
    gp = _pick(n_pages, PAGES_PER_STEP, 1)

    grid_spec = pltpu.PrefetchScalarGridSpec(
        num_scalar_prefetch=1,
        grid=(db, n_pages // gp),
        in_specs=[pl.BlockSpec((1, n_heads, ds, LATENT_PAD), lambda b, g, pt: (b, 0, 0, 0)),
                  pl.BlockSpec((1, ds, LATENT_PAD), lambda b, g, pt: (b, 0, 0)),
                  pl.BlockSpec(w_uv_pairs.shape, lambda b, g, pt: (0, 0, 0)),
                  pl.BlockSpec(memory_space=pl.ANY),
                  pl.BlockSpec(memory_space=pl.ANY)],
        out_specs=pl.BlockSpec((1, ds, n_heads * V_HEAD), lambda b, g, pt: (b, 0, 0)),
        scratch_shapes=[pltpu.VMEM((2, gp) + cache_ckv.shape[2:], cache_ckv.dtype),
                        pltpu.VMEM((2, gp) + cache_krope_t.shape[2:], cache_krope_t.dtype),
                        pltpu.SemaphoreType.DMA((2,)),
                        pltpu.VMEM((n_heads * ds, 1), F32), pltpu.VMEM((n_heads * ds, 1), F32),
                        pltpu.VMEM((n_heads * ds, KV_LORA), F32)],
    )
    return pl.pallas_call(
        functools.partial(_attn_sample_kernel, n_pages_step=gp, layer=layer),
        grid_spec=grid_spec,
        out_shape=jax.ShapeDtypeStruct((db, ds, n_heads * V_HEAD), BF16),
        compiler_params=_params("arbitrary", "arbitrary"),
        name="attention_sample",
    )(page_table, qf, kf_new, w_uv_pairs, cache_ckv, cache_krope_t)


def _merge_kernel(oa_ref, ob_ref, ga_ref, gb_ref, wa_ref, wb_ref, o_ref):
    ua = _dot(oa_ref[...], wa_ref[...])
    ub = _dot(ob_ref[...], wb_ref[...])
    o_ref[...] = (_sigmoid(ga_ref[...]) * ua + _sigmoid(gb_ref[...]) * ub).astype(o_ref.dtype)


def merge(o_a, o_b, p, ga_off, gb_off, w_up_a, w_up_b):
    n, d_a = o_a.shape
    d = w_up_a.shape[1]
    tm = _pick(n, 1376)
    tn = SEG
    return pl.pallas_call(
        _merge_kernel,
        grid=(n // tm, d // tn),
        in_specs=[pl.BlockSpec((tm, d_a), lambda i, j: (i, 0)),
                  pl.BlockSpec((tm, o_b.shape[1]), lambda i, j: (i, 0)),
                  pl.BlockSpec((tm, tn), lambda i, j: (i, ga_off // tn + j)),
                  pl.BlockSpec((tm, tn), lambda i, j: (i, gb_off // tn + j)),
                  pl.BlockSpec((d_a, tn), lambda i, j: (0, j)),
                  pl.BlockSpec((w_up_b.shape[0], tn), lambda i, j: (0, j))],
        out_specs=pl.BlockSpec((tm, tn), lambda i, j: (i, j)),
        out_shape=jax.ShapeDtypeStruct((n, d), BF16),
        compiler_params=_params("parallel", "arbitrary"),
        name="merge",
    )(o_a, o_b, p, p, w_up_a, w_up_b)


def _wo_kernel(m_ref, x_ref, w_ref, o_ref):
    o_ref[...] = x_ref[...] + _dot(m_ref[...], w_ref[...])


def out_proj(m, x, w_o):
    n, d = x.shape
    tm = _pick(n, 1376)
    tn = SEG
    return pl.pallas_call(
        _wo_kernel,
        grid=(n // tm, d // tn),
        in_specs=[pl.BlockSpec((tm, d), lambda i, j: (i, 0)),
                  pl.BlockSpec((tm, tn), lambda i, j: (i, j)),
                  pl.BlockSpec((d, tn), lambda i, j: (0, j))],
        out_specs=pl.BlockSpec((tm, tn), lambda i, j: (i, j)),
        out_shape=jax.ShapeDtypeStruct((n, d), F32),
        compiler_params=_params("parallel", "arbitrary"),
        name="out_proj",
    )(m, x, w_o)


def _router_kernel(x_ref, g_ref, w_ref, bias_ref, h_ref, r_ref):
    x = x_ref[...]
    h = x * lax.rsqrt(jnp.mean(x * x, -1, keepdims=True) + NORM_EPS) * g_ref[...]
    h_ref[...] = h.astype(h_ref.dtype)
    logits = _dot(h, w_ref[...], precision=HIGHEST)
    biased = logits + bias_ref[...]
    lane_i = lax.broadcasted_iota(jnp.int32, logits.shape, 1)
    lane = lane_i.astype(F32)
    lane_group = (lane_i // EXPERTS_PER_GROUP).astype(F32)
    is_g = (lane_i >= N_EXPERTS) & (lane_i < N_EXPERTS + N_GROUPS)
    big = float(1 << 20)

    def first_argmax(vals):
        mx = jnp.max(vals, axis=-1, keepdims=True)
        return jnp.min(jnp.where(vals == mx, lane, big), axis=-1, keepdims=True)

    def pick(vals, idx):
        return jnp.sum(jnp.where(lane == idx, vals, 0.0), axis=-1, keepdims=True)

    g_lane = first_argmax(jnp.where(is_g, biased, -jnp.inf))
    g_sel = g_lane - float(N_EXPERTS)
    g_max = jnp.max(jnp.where(is_g, logits, -jnp.inf), axis=-1, keepdims=True)
    g_exp = jnp.where(is_g, jnp.exp(logits - g_max), 0.0)
    p_sel = pick(g_exp, g_lane) / jnp.sum(g_exp, axis=-1, keepdims=True)

    in_grp = (lane_i < N_EXPERTS) & (lane_group == g_sel)
    e_biased = jnp.where(in_grp, biased, -jnp.inf)
    i1 = first_argmax(e_biased)
    i2 = first_argmax(jnp.where(lane == i1, -jnp.inf, e_biased))
    l1, l2 = pick(logits, i1), pick(logits, i2)
    mx = jnp.maximum(l1, l2)
    e1, e2 = jnp.exp(l1 - mx), jnp.exp(l2 - mx)
    w1 = e1 / (e1 + e2) * p_sel
    w2 = e2 / (e1 + e2) * p_sel
    r_ref[...] = jnp.where(lane_i == 0, i1, jnp.where(lane_i == 1, i2, jnp.where(lane_i == 2, w1,
                                                                                   jnp.where(lane_i == 3, w2, 0.0))))


def router(x, g, w_router, bias_router):
    n, d = x.shape
    tm = _pick(n, 688)
    return pl.pallas_call(
        _router_kernel,
        grid=(n // tm,),
        in_specs=[pl.BlockSpec((tm, d), lambda i: (i, 0)),
                  pl.BlockSpec((1, d), lambda i: (0, 0)),
                  pl.BlockSpec((d, LANES), lambda i: (0, 0)),
                  pl.BlockSpec((1, LANES), lambda i: (0, 0))],
        out_specs=[pl.BlockSpec((tm, d), lambda i: (i, 0)), pl.BlockSpec((tm, LANES), lambda i: (i, 0))],
        out_shape=[jax.ShapeDtypeStruct((n, d), BF16), jax.ShapeDtypeStruct((n, LANES), F32)],
        compiler_params=_params("parallel"),
        name="router",
    )(x, g, w_router, bias_router)


def _expert_kernel(te_ref, nt_ref, x_ref, gate_ref, wg_ref, wu_ref, wd_ref, o_ref, wg_bf, wu_bf, wd_bf):
    i = pl.program_id(0)

    @pl.when((i == 0) | (te_ref[i] != te_ref[jnp.maximum(i - 1, 0)]))
    def _():
        wg_bf[...] = wg_ref[...].astype(BF16)
        wu_bf[...] = wu_ref[...].astype(BF16)
        wd_bf[...] = wd_ref[...].astype(BF16)

    @pl.when(i < nt_ref[0])
    def _():
        x = x_ref[...]
        hg = _dot(x, wg_bf[...])
        hu = _dot(x, wu_bf[...])
        act = hg * _sigmoid(hg) * hu * gate_ref[...]
        o_ref[...] = _dot(act.astype(BF16), wd_bf[...])

    @pl.when(pl.program_id(0) >= nt_ref[0])
    def _():
        o_ref[...] = jnp.zeros(o_ref.shape, o_ref.dtype)


def experts(tile_expert, n_tiles_used, xg, gates, w_gate, w_up, w_down, layer):
    m_pad, d = xg.shape
    ff = w_gate.shape[-1]
    tm = EXPERT_TILE
    grid_spec = pltpu.PrefetchScalarGridSpec(
        num_scalar_prefetch=2,
        grid=(m_pad // tm,),
        in_specs=[pl.BlockSpec((tm, d), lambda i, te, nt: (i, 0)),
                  pl.BlockSpec((tm, 1), lambda i, te, nt: (i, 0)),
                  pl.BlockSpec((None, None, d, ff), lambda i, te, nt: (layer, te[i], 0, 0)),
                  pl.BlockSpec((None, None, d, ff), lambda i, te, nt: (layer, te[i], 0, 0)),
                  pl.BlockSpec((None, None, ff, d), lambda i, te, nt: (layer, te[i], 0, 0))],
        out_specs=pl.BlockSpec((tm, d), lambda i, te, nt: (i, 0)),
        scratch_shapes=[pltpu.VMEM((d, ff), BF16), pltpu.VMEM((d, ff), BF16), pltpu.VMEM((ff, d), BF16)],
    )
    return pl.pallas_call(
        _expert_kernel,
        grid_spec=grid_spec,
        out_shape=jax.ShapeDtypeStruct((m_pad, d), F32),
        compiler_params=_params("arbitrary"),
        name="experts",
    )(tile_expert, n_tiles_used, xg, gates, w_gate, w_up, w_down)


def _combine_kernel(x_ref, e0_ref, e1_ref, g_ref, o_ref):
    x = x_ref[...] + (e0_ref[...] + e1_ref[...])
    o_ref[...] = x * lax.rsqrt(jnp.mean(x * x, -1, keepdims=True) + NORM_EPS) * g_ref[...]


def combine(x, e0, e1, g, row_start):
    n, d = x.shape
    tm = _pick(int(np.gcd(n, row_start)) if row_start else n, 688)
    off = row_start // tm
    row = pl.BlockSpec((tm, d), lambda i: (i, 0))
    shifted = pl.BlockSpec((tm, d), lambda i: (i + off, 0))
    return pl.pallas_call(
        _combine_kernel,
        grid=(n // tm,),
        in_specs=[row, shifted, shifted, pl.BlockSpec((1, d), lambda i: (0, 0))],
        out_specs=row,
        out_shape=jax.ShapeDtypeStruct((n, d), F32),
        compiler_params=_params("parallel"),
        name="combine",
    )(x, e0, e1, g)


def _pad_cols(w, width):
    return jnp.pad(w, ((0, 0), (0, width - w.shape[1])))


def _pad_rows(w, height):
    return jnp.pad(w, ((0, height - w.shape[0]), (0, 0)))


def _swap_halves(w):
    half = w.shape[-1] // 2
    return jnp.concatenate([w[..., half:], w[..., :half]], axis=-1)


def _block_diag_pairs(w):
    h, a, b = w.shape
    z = jnp.zeros((h // 2, a, b), w.dtype)
    top = jnp.concatenate([w[0::2], z], axis=-1)
    bot = jnp.concatenate([z, w[1::2]], axis=-1)
    return jnp.concatenate([top, bot], axis=1)


def _rope_tables(pos):
    half = QK_ROPE // 2
    inv = ROPE_THETA ** (-jnp.arange(half, dtype=F32) / half)
    ang = pos.astype(F32)[:, None] * inv[None, :]
    cos, sin = jnp.cos(ang), jnp.sin(ang)
    cos_t = _pad_cols(jnp.concatenate([cos, cos], -1), LANES)
    sin_t = _pad_cols(jnp.concatenate([-sin, sin], -1), LANES)
    return cos_t, sin_t


def _layer_weights(l, w_in, rwkv_mu, rwkv_w0, rwkv_w2, rwkv_a0, rwkv_a2, rwkv_g2, rwkv_k_k, rwkv_k_a, rwkv_r_k,
                   rwkv_ln_g, rwkv_ln_b, mla_q_norm_g, mla_w_uq, mla_kv_norm_g, mla_w_uk, mla_w_uv, w_up_a,
                   w_up_b, w_o, norm_mix_g):
    d = w_in.shape[1]
    d_a = rwkv_w0.shape[1]
    n_heads = mla_w_uk.shape[2]
    off_kv, off_kr = Q_LORA, Q_LORA + KV_LORA
    off_rw = off_kr + QK_ROPE
    off_ga = off_rw + 3 * d_a + DECAY_LORA + AAA_LORA + GATE_LORA
    off_gb = off_ga + d
    w = w_in[l]
    rw = w[:, off_rw:off_ga]

    def lora_layout(m):
        wl = m[:, 3 * d_a:3 * d_a + DECAY_LORA]
        al = m[:, 3 * d_a + DECAY_LORA:3 * d_a + DECAY_LORA + AAA_LORA]
        gl = m[:, 3 * d_a + DECAY_LORA + AAA_LORA:]
        return jnp.concatenate([_pad_cols(wl, LANES), _pad_cols(al, LANES), gl], axis=1)

    kr = w[:, off_kr:off_rw]
    kv_seg = jnp.concatenate([w[:, off_kv:off_kr], _pad_cols(kr, LANES), _pad_cols(_swap_halves(kr), LANES)], axis=1)
    w_all = jnp.concatenate([rw[:, :3 * d_a], lora_layout(rw), w[:, off_ga:off_gb], w[:, off_gb:],
                             w[:, :Q_LORA], kv_seg], axis=1).astype(BF16)
    mu = rwkv_mu[l][None]
    mu_all = jnp.concatenate([mu[:, :3 * d_a], lora_layout(mu)], axis=1)

    uq = mla_w_uq[l].reshape(Q_LORA, n_heads, QK_NOPE + QK_ROPE)
    w_nope = uq[:, :, :QK_NOPE].reshape(Q_LORA, n_heads * QK_NOPE).astype(BF16)
    rope_cols = uq[:, :, QK_NOPE:]
    pad_heads = lambda m: jnp.pad(m, ((0, 0), (0, 0), (0, LANES - QK_ROPE))).reshape(Q_LORA, n_heads * LANES)
    w_rope = pad_heads(rope_cols).astype(BF16)
    w_rope_sw = pad_heads(_swap_halves(rope_cols)).astype(BF16)
    w_uk_pairs = _block_diag_pairs(jnp.transpose(mla_w_uk[l], (1, 2, 0))).astype(BF16)
    w_uv_pairs = _block_diag_pairs(jnp.transpose(mla_w_uv[l], (1, 0, 2))).astype(BF16)

    return dict(
        w_all=w_all, mu_all=mu_all, norm_mix_g=norm_mix_g[l][None],
        ga_off=3 * d_a + SEG, gb_off=3 * d_a + SEG + d,
        w0=rwkv_w0[l][None], a0=rwkv_a0[l][None], k_k=rwkv_k_k[l][None], k_a=rwkv_k_a[l][None],
        w2=_pad_rows(rwkv_w2[l], LANES).astype(BF16), a2=_pad_rows(rwkv_a2[l], LANES).astype(BF16),
        g2=rwkv_g2[l].astype(BF16),
        r_k=rwkv_r_k[l].reshape(1, d_a), ln_g=rwkv_ln_g[l][None], ln_b=rwkv_ln_b[l][None],
        g_q=mla_q_norm_g[l][None], g_kv=mla_kv_norm_g[l][None],
        w_nope=w_nope, w_rope=w_rope, w_rope_sw=w_rope_sw, w_uk_pairs=w_uk_pairs, w_uv_pairs=w_uv_pairs,
        w_up_a=w_up_a[l].astype(BF16), w_up_b=w_up_b[l].astype(BF16), w_o=w_o[l].astype(BF16),
        n_heads=n_heads, d_a=d_a, lora_layout=lora_layout,
    )


def _token_mixer(x3, pos, shift_prev, wkv_prev, lw, attend):
    b, t, d = x3.shape
    n = b * t
    d_a = lw["d_a"]
    x2 = x3.reshape(n, d)
    p = rms_matmul(x2, lw["norm_mix_g"], lw["w_all"])
    p3 = p.reshape(b, t, p.shape[1])

    chunk = _pick(t, SCAN_CHUNK)
    if t >= 344:
        bb, tt = 1, _pick(t, 344)
        tt_prep = _pick(t, 688, chunk)
    else:
        bb, tt = _pick(b, max(1, 256 // t), 1), t
        tt_prep = t

    def column_tiles(m):
        n_ct = d_a // SEG
        parts = [m[:, part * d_a:(part + 1) * d_a].reshape(-1, n_ct, SEG) for part in range(3)]
        lora = jnp.broadcast_to(m[:, None, 3 * d_a:], (m.shape[0], n_ct, SEG))
        return jnp.concatenate(parts + [lora], axis=-1)[:, :, None, :]

    shift_cols = jnp.concatenate([shift_prev[:, :3 * d_a], lw["lora_layout"](shift_prev)], axis=1)
    rt, qu, kh, bh, kt, bt, v, gam, bonus, g = rwkv_prepare(
        p3, column_tiles(shift_cols), column_tiles(lw["mu_all"])[0], lw["w0"], lw["a0"], lw["k_k"], lw["k_a"],
        lw["r_k"], lw["w2"], lw["a2"], lw["g2"], bb, tt_prep, chunk)
    y, s_new = rwkv_scan(rt, qu, kh, bh, kt, bt, v, gam, wkv_prev, chunk)
    flat = lambda a: a.reshape(n, d_a)
    o_a = rwkv_out(flat(y), flat(bonus), flat(g), lw["ln_g"], lw["ln_b"])

    cos_t, sin_t = _rope_tables(pos)
    qf, kf, ckv, krope = mla_project(p3, cos_t, sin_t, lw["g_q"], lw["g_kv"], lw["w_nope"], lw["w_rope"],
                                     lw["w_rope_sw"], lw["w_uk_pairs"], lw["n_heads"], bb, tt)
    o_b = attend(qf, kf).reshape(n, -1)

    m = merge(o_a, o_b, p, lw["ga_off"], lw["gb_off"], lw["w_up_a"], lw["w_up_b"])
    x_new = out_proj(m, x2, lw["w_o"])

    last = p3[:, -1, :]
    shift_new = jnp.concatenate([last[:, :3 * d_a],
                                 last[:, 3 * d_a:3 * d_a + DECAY_LORA],
                                 last[:, 3 * d_a + LANES:3 * d_a + LANES + AAA_LORA],
                                 last[:, 3 * d_a + 2 * LANES:3 * d_a + SEG]], axis=1)
    return x_new, (ckv, krope, s_new, shift_new)


def _moe(x_rows, norm_g, router_group_w, router_group_b, router_expert_w, router_expert_b,
         w_gate, w_up, w_down, layer, final_g):
    w_router = _pad_cols(jnp.concatenate([router_expert_w, router_group_w], axis=1), LANES)
    b_router = _pad_cols(jnp.concatenate([router_expert_b, router_group_b])[None], LANES)
    hs, routes = zip(*[router(x, norm_g, w_router, b_router) for x in x_rows])
    h = jnp.concatenate(hs, axis=0)
    route = jnp.concatenate(routes, axis=0)
    n = h.shape[0]
    eid = route[:, :2].astype(jnp.int32).reshape(-1)
    gate = route[:, 2:4].reshape(-1)

    tm = EXPERT_TILE
    n_tiles = -(-(2 * n + N_EXPERTS * (tm - 1)) // tm)
    m_pad = n_tiles * tm
    experts_iota = jnp.arange(N_EXPERTS, dtype=jnp.int32)
    order = jnp.argsort(eid, stable=True).astype(jnp.int32)
    rank = jnp.argsort(order).astype(jnp.int32)
    counts = jnp.sum((eid[:, None] == experts_iota[None, :]).astype(jnp.int32), axis=0)
    padded = ((counts + tm - 1) // tm) * tm
    seg_end = jnp.cumsum(padded)
    seg_start = seg_end - padded
    cnt_start = jnp.cumsum(counts) - counts
    dest = jnp.take(seg_start - cnt_start, eid, mode="clip") + rank
    tile_start = jnp.arange(n_tiles, dtype=jnp.int32) * tm
    tile_expert = jnp.minimum(jnp.sum((seg_end[None, :] <= tile_start[:, None]).astype(jnp.int32), axis=1),
                              N_EXPERTS - 1)
    n_used = (seg_end[-1] // tm).astype(jnp.int32).reshape(1)
    row_expert = jnp.repeat(tile_expert, tm)
    within = jnp.arange(m_pad, dtype=jnp.int32) - jnp.take(seg_start, row_expert, mode="clip")
    valid = within < jnp.take(counts, row_expert, mode="clip")
    row_assign = jnp.take(order, jnp.take(cnt_start, row_expert, mode="clip") + within, mode="clip")
    row_token = jnp.where(valid, row_assign // 2, 0)
    row_gate = jnp.where(valid, jnp.take(gate, row_assign, mode="clip"), 0.0)

    xg = jnp.take(h, row_token, axis=0, mode="clip")
    eo = experts(tile_expert, n_used, xg, row_gate[:, None], w_gate, w_up, w_down, layer)
    dest2 = dest.reshape(n, 2)
    e0 = jnp.take(eo, dest2[:, 0], axis=0, mode="clip")
    e1 = jnp.take(eo, dest2[:, 1], axis=0, mode="clip")

    outs, start = [], 0
    for x in x_rows:
        outs.append(combine(x, e0, e1, final_g, start))
        start += x.shape[0]
    return outs


def kernel(x_prompt, x_sample, cache_ckv, cache_krope, state_wkv, state_shift, page_table, meta_tokens, norm_mix_g, w_in, rwkv_mu, rwkv_w0, rwkv_w2, rwkv_a0, rwkv_a2, rwkv_g2, rwkv_k_k, rwkv_k_a, rwkv_r_k, rwkv_ln_g, rwkv_ln_b, mla_q_norm_g, mla_w_uq, mla_kv_norm_g, mla_w_uk, mla_w_uv, w_up_a, w_up_b, w_o, norm_ffn_g, router_group_w, router_group_b, router_expert_w, router_expert_b, expert_w_gate, expert_w_up, expert_w_down, norm_final_g):
    depth = w_in.shape[0]
    assert depth == 1, "the final norm is fused into the MoE combine, which assumes a single layer"
    b, s, d = x_prompt.shape
    db, ds, _ = x_sample.shape
    past = page_table.shape[1] * PAGE_SIZE
    t = N_META + s
    n_heads_a = state_wkv.shape[2]
    rwkv_cols = state_shift.shape[-1]

    xp = jnp.concatenate([jnp.broadcast_to(meta_tokens[None], (b, N_META, d)), x_prompt], axis=1)
    l = 0
    lw = _layer_weights(l, w_in, rwkv_mu, rwkv_w0, rwkv_w2, rwkv_a0, rwkv_a2, rwkv_g2, rwkv_k_k, rwkv_k_a,
                        rwkv_r_k, rwkv_ln_g, rwkv_ln_b, mla_q_norm_g, mla_w_uq, mla_kv_norm_g, mla_w_uk,
                        mla_w_uv, w_up_a, w_up_b, w_o, norm_mix_g)

    xp1, st_p = _token_mixer(xp, jnp.arange(t), jnp.zeros((b, rwkv_cols), F32),
                             jnp.zeros((b, n_heads_a, HEAD_A, HEAD_A), F32), lw,
                             lambda qf, kf: attention_prompt(qf, kf, lw["w_uv_pairs"]))
    xs1, st_s = _token_mixer(x_sample, past + jnp.arange(ds), state_shift[l], state_wkv[l], lw,
                             lambda qf, kf: attention_sample(qf, kf, cache_ckv, jnp.swapaxes(cache_krope, 2, 3), l,
                                                             page_table, lw["w_uv_pairs"]))

    yp, ys = _moe([xp1, xs1], norm_ffn_g[l][None], router_group_w[l], router_group_b[l], router_expert_w[l],
                  router_expert_b[l], expert_w_gate, expert_w_up, expert_w_down, l, norm_final_g[None])
    y_prompt = yp.reshape(b, t, d)[:, N_META:]
    y_sample = ys.reshape(db, ds, d)
    return (y_prompt, y_sample,
            st_p[0][None], st_p[1][None], st_p[2][None], st_p[3][None],
            st_s[0][None], st_s[1][None], st_s[2][None], st_s[3][None])
```

```python
import functools

import jax
import jax.numpy as jnp
import numpy as np
from jax import lax
from jax.experimental import pallas as pl
from jax.experimental.pallas import tpu as pltpu

F32 = jnp.float32
BF16 = jnp.bfloat16
HIGHEST = lax.Precision.HIGHEST

N_META = 16
NORM_EPS = 1e-6
NEG = -1e30
HEAD_A = 64
DECAY_LORA = 96
AAA_LORA = 96
GATE_LORA = 256
LNX_EPS = 64e-5
QK_NOPE = 64
QK_ROPE = 32
V_HEAD = 64
Q_LORA = 512
KV_LORA = 256
ROPE_THETA = 10000.0
PAGE_SIZE = 128
N_GROUPS = 4
EXPERTS_PER_GROUP = 8
N_EXPERTS = N_GROUPS * EXPERTS_PER_GROUP

LANES = 128
HEAD_PAIR = 2 * HEAD_A
SEG = 512
LATENT_PAD = KV_LORA + LANES
VMEM_LIMIT = 56 * 1024 * 1024
SCAN_CHUNK = 16
SCAN_BATCH = 4
EXPERT_TILE = 256
PAGES_PER_STEP = 32
SAMPLE_SUBGROUP = 8
ATTN_HEAD_GROUP = 2


def _pick(n, target, mult=8):
    best = None
    for d in range(mult, min(n, target) + 1, mult):
        if n % d == 0:
            best = d
    return best if best is not None else n


def _params(*sem):
    return pltpu.CompilerParams(dimension_semantics=sem, vmem_limit_bytes=VMEM_LIMIT)


def _dot(a, b, **kw):
    return jnp.dot(a, b, preferred_element_type=F32, **kw)


def _dot_nt(a, b):
    return lax.dot_general(a, b, (((1,), (1,)), ((), ())), preferred_element_type=F32)


def _dot_tn(a, b):
    return lax.dot_general(a, b, (((0,), (0,)), ((), ())), preferred_element_type=F32)


def _sigmoid(x):
    return 1.0 / (1.0 + jnp.exp(-x))


def _head_ones(scale=1.0):
    r = lax.broadcasted_iota(jnp.int32, (LANES, LANES), 0) // HEAD_A
    c = lax.broadcasted_iota(jnp.int32, (LANES, LANES), 1) // HEAD_A
    return jnp.where(r == c, scale, 0.0).astype(F32)


def _rms_mm_kernel(x_ref, g_ref, w_ref, o_ref, h_ref):
    @pl.when(pl.program_id(1) == 0)
    def _():
        x = x_ref[...]
        ms = jnp.mean(x * x, axis=-1, keepdims=True)
        h_ref[...] = (x * lax.rsqrt(ms + NORM_EPS) * g_ref[...]).astype(BF16)

    o_ref[...] = _dot(h_ref[...], w_ref[...])


def rms_matmul(x, g, w):
    n, k = x.shape
    m = w.shape[1]
    tm = _pick(n, 1376)
    tn = SEG
    return pl.pallas_call(
        _rms_mm_kernel,
        grid=(n // tm, m // tn),
        in_specs=[pl.BlockSpec((tm, k), lambda i, j: (i, 0)),
                  pl.BlockSpec((1, k), lambda i, j: (0, 0)),
                  pl.BlockSpec((k, tn), lambda i, j: (0, j))],
        out_specs=pl.BlockSpec((tm, tn), lambda i, j: (i, j)),
        out_shape=jax.ShapeDtypeStruct((n, m), F32),
        scratch_shapes=[pltpu.VMEM((tm, k), BF16)],
        compiler_params=_params("parallel", "arbitrary"),
        name="rms_matmul",
    )(x, g, w)


def _prepare_kernel(pr_ref, pk_ref, pv_ref, pl_ref, sh_ref, mu_ref, w0_ref, a0_ref, kk_ref, ka_ref, rk_ref,
                    w2_ref, a2_ref, g2_ref,
                    rt_out, qu_out, kh_out, bh_out, kt_out, bt_out, v_out, gam_out, bonus_out, g_out,
                    carry_ref, *, chunk):
    tc = pr_ref.shape[-1]
    ct = pl.program_id(2)

    @pl.when(pl.program_id(1) == 0)
    def _():
        carry_ref[ct] = sh_ref[...]

    def shifted(ref, lo, width):
        x = ref[...]
        prev_row = carry_ref[ct, :, :, lo:lo + width]
        t_idx = lax.broadcasted_iota(jnp.int32, x.shape, 1)
        prev = jnp.where(t_idx == 0, prev_row, pltpu.roll(x, 1, axis=1))
        carry_ref[ct, :, :, lo:lo + width] = x[:, x.shape[1] - 1:, :]
        xs = x + (prev - x) * mu_ref[:, lo:lo + width]
        return xs.reshape(x.shape[0] * x.shape[1], width)

    bb, tt, _ = pr_ref.shape
    rows = bb * tt
    r = shifted(pr_ref, 0, tc)
    k = shifted(pk_ref, tc, tc)
    v = shifted(pv_ref, 2 * tc, tc)
    lora = shifted(pl_ref, 3 * tc, SEG)
    wl, al, gl = lora[:, :LANES], lora[:, LANES:2 * LANES], lora[:, 2 * LANES:]

    z = -(w0_ref[...] + _dot(jnp.tanh(wl).astype(BF16), w2_ref[...]))
    softplus = jnp.maximum(z, 0.0) + jnp.log(1.0 + jnp.exp(-jnp.abs(z)))
    lw = -jnp.exp(-softplus - 0.5)
    a = _sigmoid(a0_ref[...] + _dot(al.astype(BF16), a2_ref[...]))
    g = _dot(_sigmoid(gl).astype(BF16), g2_ref[...])

    kkf = k * kk_ref[...]
    ones = _head_ones()
    per_head = lambda x: jnp.concatenate(
        [_dot(x[:, j * LANES:(j + 1) * LANES], ones, precision=HIGHEST) for j in range(tc // LANES)], axis=-1)
    kk = kkf * lax.rsqrt(jnp.maximum(per_head(jnp.square(kkf)), 1e-24))
    k_mod = k * (1.0 + (a - 1.0) * ka_ref[...])
    b = kk * a
    bonus = per_head(r * k_mod * rk_ref[...]) * v

    pos = lax.broadcasted_iota(jnp.int32, (rows, tc), 0) % chunk
    cum = lw
    sh = 1
    while sh < chunk:
        cum = cum + jnp.where(pos >= sh, pltpu.roll(cum, sh, axis=0), 0.0)
        sh *= 2
    total = jnp.where(pos == chunk - 1, cum, 0.0)
    sh = 1
    while sh < chunk:
        total = total + pltpu.roll(total, rows - sh, axis=0)
        sh *= 2
    e_neg = jnp.exp(-cum)
    e_tail = jnp.exp(total - cum)

    def st(ref, val):
        ref[...] = val.reshape(bb, tt, tc).astype(ref.dtype)

    st(rt_out, r * jnp.exp(cum))
    st(qu_out, kk * jnp.exp(cum - lw))
    st(kh_out, k_mod * e_neg)
    st(bh_out, b * e_neg)
    st(kt_out, k_mod * e_tail)
    st(bt_out, b * e_tail)
    st(v_out, v)
    st(bonus_out, bonus)
    st(g_out, g)
    gam_out[...] = jnp.exp(total).reshape(bb, tt // chunk, chunk, tc)[:, :, 0:1, :]


def rwkv_prepare(p3, shift4, mu3, w0, a0, k_k, k_a, r_k, w2, a2, g2, bb, tt, chunk):
    b, t, _ = p3.shape
    d_a = w0.shape[-1]
    tc = SEG
    n_ct = d_a // tc
    width = 3 * tc + SEG
    row = lambda part: pl.BlockSpec((bb, tt, tc), lambda i, j, c, part=part: (i, j, part * n_ct + c))
    vec = pl.BlockSpec((1, tc), lambda i, j, c: (0, c))
    mat = lambda a: pl.BlockSpec((a.shape[0], tc), lambda i, j, c: (0, c))
    out_spec = pl.BlockSpec((bb, tt, tc), lambda i, j, c: (i, j, c))
    seq = lambda dt: jax.ShapeDtypeStruct((b, t, d_a), dt)
    return pl.pallas_call(
        functools.partial(_prepare_kernel, chunk=chunk),
        grid=(b // bb, t // tt, n_ct),
        in_specs=[row(0), row(1), row(2),
                  pl.BlockSpec((bb, tt, SEG), lambda i, j, c: (i, j, 3 * d_a // SEG)),
                  pl.BlockSpec((bb, None, 1, width), lambda i, j, c: (i, c, 0, 0)),
                  pl.BlockSpec((None, 1, width), lambda i, j, c: (c, 0, 0)),
                  vec, vec, vec, vec, vec, mat(w2), mat(a2), mat(g2)],
        out_specs=[out_spec] * 7
                  + [pl.BlockSpec((bb, tt // chunk, 1, tc), lambda i, j, c: (i, j, 0, c)), out_spec, out_spec],
        out_shape=[seq(BF16)] * 7 + [jax.ShapeDtypeStruct((b, t // chunk, 1, d_a), F32), seq(F32), seq(F32)],
        scratch_shapes=[pltpu.VMEM((n_ct, bb, 1, width), F32)],
        compiler_params=_params("parallel", "arbitrary", "arbitrary"),
        name="rwkv_prepare",
    )(p3, p3, p3, p3, shift4, mu3, w0, a0, k_k, k_a, r_k, w2, a2, g2)


def _scan_kernel(rt_ref, qu_ref, kh_ref, bh_ref, kt_ref, bt_ref, v_ref, gam_ref, s0_ref, y_ref, sf_ref, st_ref,
                 *, chunk):
    c_idx = pl.program_id(1)
    nb, n_pairs = st_ref.shape[:2]
    chains = [(bi, j) for bi in range(nb) for j in range(n_pairs)]
    c2x = 2 * chunk

    @pl.when(c_idx == 0)
    def _():
        zero = jnp.zeros((HEAD_A, HEAD_A), F32)
        for bi, j in chains:
            top = jnp.concatenate([s0_ref[bi, 2 * j], zero], axis=1)
            bot = jnp.concatenate([zero, s0_ref[bi, 2 * j + 1]], axis=1)
            st_ref[bi, j] = jnp.concatenate([top, bot], axis=0)

    r2 = lax.broadcasted_iota(jnp.int32, (c2x, c2x), 0)
    c2 = lax.broadcasted_iota(jnp.int32, (c2x, c2x), 1)
    strict = c2 < r2
    incl = c2 <= r2
    head0 = lax.broadcasted_iota(jnp.int32, (chunk, LANES), 1) < HEAD_A

    def stack2(ref, bi, j):
        x = ref[bi, :, j * LANES:(j + 1) * LANES].astype(F32)
        return jnp.concatenate([jnp.where(head0, x, 0.0), jnp.where(head0, 0.0, x)], axis=0).astype(BF16)

    each = lambda f, *lists: [f(*args) for args in zip(*lists)]
    s_prev = [st_ref[bi, j] for bi, j in chains]
    gam = [gam_ref[bi, 0, :, j * LANES:(j + 1) * LANES] for bi, j in chains]
    q_u, q_r, k_h, b_h, k_t, b_t, v2 = [[stack2(ref, bi, j) for bi, j in chains]
                                        for ref in (qu_ref, rt_ref, kh_ref, bh_ref, kt_ref, bt_ref, v_ref)]
    qq = each(lambda a, b: jnp.concatenate([a, b], axis=0), q_u, q_r)
    s_bf = [s.astype(BF16) for s in s_prev]
    sk = each(_dot_nt, qq, k_h)
    sb = each(_dot_nt, qq, b_h)
    ss = each(_dot_nt, qq, s_bf)
    a_vk = [jnp.where(strict, m[:c2x], 0.0).astype(BF16) for m in sk]
    power = [jnp.where(strict, -m[:c2x], 0.0) for m in sb]
    a_rk = [jnp.where(incl, m[c2x:], 0.0).astype(BF16) for m in sk]
    a_rb = [jnp.where(incl, m[c2x:], 0.0).astype(BF16) for m in sb]
    x = each(lambda s, a, v: s[:c2x] + _dot(a, v), ss, a_vk, v2)
    steps = max(1, int(np.ceil(np.log2(chunk))))
    for s in range(steps):
        pb = [p.astype(BF16) for p in power]
        x = each(lambda xi, p: xi + _dot(p, xi.astype(BF16)), x, pb)
        if s + 1 < steps:
            power = each(_dot, pb, pb)
    u2 = [xi.astype(BF16) for xi in x]
    y2 = each(lambda s, ak, v, ab, u: s[c2x:] + _dot(ak, v) - _dot(ab, u), ss, a_rk, v2, a_rb, u2)
    s_new = each(lambda s, g, v, u, kt, bt: s * g + _dot_tn(jnp.concatenate([v, -u], axis=0),
                                                              jnp.concatenate([kt, bt], axis=0)),
                 s_prev, gam, v2, u2, k_t, b_t)

    for (bi, j), y, s in zip(chains, y2, s_new):
        y_ref[bi, :, j * LANES:(j + 1) * LANES] = y[:chunk] + y[chunk:]
        st_ref[bi, j] = s

    @pl.when(c_idx == pl.num_programs(1) - 1)
    def _():
        for (bi, j), s in zip(chains, s_new):
            sf_ref[bi, 2 * j] = s[:HEAD_A, :HEAD_A]
            sf_ref[bi, 2 * j + 1] = s[HEAD_A:, HEAD_A:]


def rwkv_scan(rt, qu, kh, bh, kt, bt, v, gam, s0, chunk):
    bsz, t, d_a = rt.shape
    n_pairs = d_a // LANES
    nb = SCAN_BATCH if bsz % SCAN_BATCH == 0 else 1
    seq = pl.BlockSpec((nb, chunk, d_a), lambda i, c: (i, c, 0))
    st = pl.BlockSpec((nb, 2 * n_pairs, HEAD_A, HEAD_A), lambda i, c: (i, 0, 0, 0))
    return pl.pallas_call(
        functools.partial(_scan_kernel, chunk=chunk),
        grid=(bsz // nb, t // chunk),
        in_specs=[seq] * 7 + [pl.BlockSpec((nb, 1, 1, d_a), lambda i, c: (i, c, 0, 0)), st],
        out_specs=[seq, st],
        out_shape=[jax.ShapeDtypeStruct((bsz, t, d_a), F32),
                   jax.ShapeDtypeStruct((bsz, 2 * n_pairs, HEAD_A, HEAD_A), F32)],
        scratch_shapes=[pltpu.VMEM((nb, n_pairs, LANES, LANES), F32)],
        compiler_params=_params("parallel", "arbitrary"),
        name="rwkv_scan",
    )(rt, qu, kh, bh, kt, bt, v, gam, s0)


def _rwkv_out_kernel(y_ref, bonus_ref, g_ref, lng_ref, lnb_ref, o_ref):
    mean_m = _head_ones(1.0 / HEAD_A)
    for j in range(y_ref.shape[-1] // LANES):
        sl = slice(j * LANES, (j + 1) * LANES)
        y = y_ref[:, sl]
        d = y - _dot(y, mean_m, precision=HIGHEST)
        var = _dot(d * d, mean_m, precision=HIGHEST)
        yn = d * lax.rsqrt(var + LNX_EPS) * lng_ref[:, sl] + lnb_ref[:, sl]
        o_ref[:, sl] = ((yn + bonus_ref[:, sl]) * g_ref[:, sl]).astype(o_ref.dtype)


def rwkv_out(y, bonus, g, ln_g, ln_b):
    n, d_a = y.shape
    tm = _pick(n, 688)
    row = pl.BlockSpec((tm, d_a), lambda i: (i, 0))
    vec = pl.BlockSpec((1, d_a), lambda i: (0, 0))
    return pl.pallas_call(
        _rwkv_out_kernel,
        grid=(n // tm,),
        in_specs=[row] * 3 + [vec] * 2,
        out_specs=row,
        out_shape=jax.ShapeDtypeStruct((n, d_a), BF16),
        compiler_params=_params("parallel"),
        name="rwkv_out",
    )(y, bonus, g, ln_g, ln_b)


def _mla_kernel(pq_ref, pkv_ref, cos_ref, sin_ref, gq_ref, gkv_ref, wn_ref, wr_ref, wrs_ref, wuk_ref,
                qf_ref, kf_ref, ckv_ref, kr_ref, *, scale):
    bb, tt, _ = pq_ref.shape
    rows = bb * tt
    n_heads = qf_ref.shape[1]
    cos = jnp.broadcast_to(cos_ref[...][None], (bb, tt, LANES)).reshape(rows, LANES)
    sin = jnp.broadcast_to(sin_ref[...][None], (bb, tt, LANES)).reshape(rows, LANES)

    pq = pq_ref[...].reshape(rows, Q_LORA)
    c_q = (pq * lax.rsqrt(jnp.mean(pq * pq, -1, keepdims=True) + NORM_EPS) * gq_ref[...]).astype(BF16)
    q_nope = _dot(c_q, wn_ref[...]).astype(BF16)
    q_rope = _dot(c_q, wr_ref[...])
    q_rope_sw = _dot(c_q, wrs_ref[...])
    for j in range(n_heads // 2):
        lat = _dot(q_nope[:, j * LANES:(j + 1) * LANES], wuk_ref[j]) * scale
        for hh in range(2):
            h = 2 * j + hh
            qf_ref[:, h, :, :KV_LORA] = lat[:, hh * KV_LORA:(hh + 1) * KV_LORA].reshape(bb, tt, KV_LORA).astype(BF16)
            sl = slice(h * LANES, (h + 1) * LANES)
            roped = (q_rope[:, sl] * cos + q_rope_sw[:, sl] * sin) * scale
            qf_ref[:, h, :, KV_LORA:] = roped.reshape(bb, tt, LANES).astype(BF16)

    pkv = pkv_ref[...].reshape(rows, SEG)
    kv = pkv[:, :KV_LORA]
    ckv = kv * lax.rsqrt(jnp.mean(kv * kv, -1, keepdims=True) + NORM_EPS) * gkv_ref[...]
    krope = pkv[:, KV_LORA:KV_LORA + LANES] * cos + pkv[:, KV_LORA + LANES:] * sin
    ckv_ref[...] = ckv.reshape(bb, tt, KV_LORA)
    kr_ref[...] = krope[:, :QK_ROPE].reshape(bb, tt, QK_ROPE)
    kf_ref[:, :, :KV_LORA] = ckv.reshape(bb, tt, KV_LORA).astype(BF16)
    kf_ref[:, :, KV_LORA:] = krope.reshape(bb, tt, LANES).astype(BF16)


def mla_project(p3, cos_t, sin_t, g_q, g_kv, w_nope, w_rope, w_rope_sw, w_uk_pairs, n_heads, bb, tt):
    b, t, width = p3.shape
    q_blk = (width - 2 * SEG) // SEG
    full = lambda a: pl.BlockSpec(a.shape, lambda i, j: (0,) * a.ndim)
    scale = float((QK_NOPE + QK_ROPE) ** -0.5)
    return pl.pallas_call(
        functools.partial(_mla_kernel, scale=scale),
        grid=(b // bb, t // tt),
        in_specs=[pl.BlockSpec((bb, tt, SEG), lambda i, j: (i, j, q_blk)),
                  pl.BlockSpec((bb, tt, SEG), lambda i, j: (i, j, q_blk + 1)),
                  pl.BlockSpec((tt, LANES), lambda i, j: (j, 0)),
                  pl.BlockSpec((tt, LANES), lambda i, j: (j, 0)),
                  full(g_q), full(g_kv), full(w_nope), full(w_rope), full(w_rope_sw), full(w_uk_pairs)],
        out_specs=[pl.BlockSpec((bb, n_heads, tt, LATENT_PAD), lambda i, j: (i, 0, j, 0)),
                   pl.BlockSpec((bb, tt, LATENT_PAD), lambda i, j: (i, j, 0)),
                   pl.BlockSpec((bb, tt, KV_LORA), lambda i, j: (i, j, 0)),
                   pl.BlockSpec((bb, tt, QK_ROPE), lambda i, j: (i, j, 0))],
        out_shape=[jax.ShapeDtypeStruct((b, n_heads, t, LATENT_PAD), BF16),
                   jax.ShapeDtypeStruct((b, t, LATENT_PAD), BF16),
                   jax.ShapeDtypeStruct((b, t, KV_LORA), F32),
                   jax.ShapeDtypeStruct((b, t, QK_ROPE), F32)],
        compiler_params=_params("parallel", "parallel"),
        name="mla_project",
    )(p3, p3, cos_t, sin_t, g_q, g_kv, w_nope, w_rope, w_rope_sw, w_uk_pairs)


def _lane_tiles(x, width):
    return jnp.concatenate([x] * (-(-width // LANES)), axis=1)[:, :width]


def _value_up(o_lat, wuv_ref, store):
    n_heads = o_lat.shape[0]
    for j in range(n_heads // 2):
        lhs = jnp.concatenate([o_lat[2 * j], o_lat[2 * j + 1]], axis=-1).astype(BF16)
        store(j, _dot(lhs, wuv_ref[j]))


def _attn_prompt_kernel(q_ref, k_ref, wuv_ref, o_ref, m_ref, l_ref, acc_ref, *, tq):
    qi, ki = pl.program_id(1), pl.program_id(2)
    n_heads = q_ref.shape[1]
    rows = n_heads * tq

    @pl.when(ki == 0)
    def _():
        m_ref[...] = jnp.full(m_ref.shape, NEG, F32)
        l_ref[...] = jnp.zeros(l_ref.shape, F32)
        acc_ref[...] = jnp.zeros(acc_ref.shape, F32)

    def step(diagonal):
        k = k_ref[0]
        hg = ATTN_HEAD_GROUP if n_heads % ATTN_HEAD_GROUP == 0 else n_heads
        grows = hg * tq

        def scores(g):
            return _dot_nt(q_ref[0, g * hg:(g + 1) * hg].reshape(grows, LATENT_PAD), k)

        def softmax_pv(g, s):
            sl = slice(g * grows, (g + 1) * grows)
            if diagonal:
                causal = (lax.broadcasted_iota(jnp.int32, (tq, tq), 1)
                          <= lax.broadcasted_iota(jnp.int32, (tq, tq), 0))
                s = jnp.where(causal[None], s.reshape(hg, tq, tq), NEG).reshape(grows, tq)
            m_prev = m_ref[sl]
            m_new = jnp.maximum(m_prev, jnp.max(s, axis=-1, keepdims=True))
            corr = jnp.exp(m_prev - m_new)
            p = jnp.exp(s - _lane_tiles(m_new, tq))
            l_ref[sl] = l_ref[sl] * corr + jnp.sum(p, axis=-1, keepdims=True)
            acc_ref[sl] = acc_ref[sl] * _lane_tiles(corr, KV_LORA) + _dot(p.astype(BF16), k[:, :KV_LORA])
            m_ref[sl] = m_new

        n_groups = n_heads // hg
        s_next = scores(0)
        for g in range(n_groups):
            s_cur = s_next
            if g + 1 < n_groups:
                s_next = scores(g + 1)
            softmax_pv(g, s_cur)

    @pl.when(ki < qi)
    def _():
        step(False)

    @pl.when(ki == qi)
    def _():
        step(True)
        o_lat = (acc_ref[...] / _lane_tiles(l_ref[...], KV_LORA)).reshape(n_heads, tq, KV_LORA)

        def store(j, val):
            o_ref[0, :, j * LANES:(j + 1) * LANES] = val.astype(o_ref.dtype)
        _value_up(o_lat, wuv_ref, store)


def attention_prompt(qf, kf, w_uv_pairs):
    b, n_heads, t, _ = qf.shape
    tq = _pick(t, 344)
    nq = t // tq
    return pl.pallas_call(
        functools.partial(_attn_prompt_kernel, tq=tq),
        grid=(b, nq, nq),
        in_specs=[pl.BlockSpec((1, n_heads, tq, LATENT_PAD), lambda i, q, k: (i, 0, q, 0)),
                  pl.BlockSpec((1, tq, LATENT_PAD), lambda i, q, k: (i, jnp.minimum(k, q), 0)),
                  pl.BlockSpec(w_uv_pairs.shape, lambda i, q, k: (0, 0, 0))],
        out_specs=pl.BlockSpec((1, tq, n_heads * V_HEAD), lambda i, q, k: (i, q, 0)),
        out_shape=jax.ShapeDtypeStruct((b, t, n_heads * V_HEAD), BF16),
        scratch_shapes=[pltpu.VMEM((n_heads * tq, LANES), F32), pltpu.VMEM((n_heads * tq, LANES), F32),
                        pltpu.VMEM((n_heads * tq, KV_LORA), F32)],
        compiler_params=_params("parallel", "parallel", "arbitrary"),
        name="attention_prompt",
    )(qf, kf, w_uv_pairs)


def _attn_sample_kernel(pt_ref, q_ref, kn_ref, wuv_ref, ckv_hbm, kr_hbm, o_ref, kbuf, rbuf, sem, m_ref, l_ref,
                        acc_ref, *, n_pages_step, layer):
    b, g = pl.program_id(0), pl.program_id(1)
    n_groups = pl.num_programs(1)
    step = b * n_groups + g
    slot = step % 2
    n_heads, ds = q_ref.shape[1], q_ref.shape[2]
    rows = n_heads * ds

    def page_copies(bb, gg, sl):
        copies = []
        for i in range(n_pages_step):
            page = pt_ref[bb, gg * n_pages_step + i]
            copies.append(pltpu.make_async_copy(ckv_hbm.at[layer, page], kbuf.at[sl, i], sem.at[sl]))
            copies.append(pltpu.make_async_copy(kr_hbm.at[layer, page], rbuf.at[sl, i], sem.at[sl]))
        return copies

    @pl.when(step == 0)
    def _():
        for c in page_copies(b, g, slot):
            c.start()

    @pl.when(step + 1 < pl.num_programs(0) * n_groups)
    def _():
        wrap = g + 1 == n_groups
        for c in page_copies(jnp.where(wrap, b + 1, b), jnp.where(wrap, 0, g + 1), 1 - slot):
            c.start()

    for c in page_copies(b, g, slot):
        c.wait()
    ckv_refs = [kbuf.at[slot, i] for i in range(n_pages_step)]
    kr_refs = [rbuf.at[slot, i] for i in range(n_pages_step)]

    @pl.when(g == 0)
    def _():
        m_ref[...] = jnp.full(m_ref.shape, NEG, F32)
        l_ref[...] = jnp.zeros(l_ref.shape, F32)
        acc_ref[...] = jnp.zeros(acc_ref.shape, F32)

    q = q_ref[0].reshape(rows, LATENT_PAD)
    q_lat, q_rope = q[:, :KV_LORA], q[:, KV_LORA:KV_LORA + QK_ROPE]

    def update(state, scores, values):
        m_prev, l_prev, acc = state
        s_max = functools.reduce(jnp.maximum, scores)
        m_new = jnp.maximum(m_prev, jnp.max(s_max, axis=-1, keepdims=True))
        corr = jnp.exp(m_prev - m_new)
        m_wide = _lane_tiles(m_new, scores[0].shape[1])
        probs = [jnp.exp(s - m_wide) for s in scores]
        p_sum = functools.reduce(jnp.add, probs)
        acc = acc * _lane_tiles(corr, KV_LORA)
        for p, val in zip(probs, values):
            acc = acc + _dot(p.astype(BF16), val)
        return m_new, l_prev * corr + jnp.sum(p_sum, axis=-1, keepdims=True), acc

    def scores_of(lo, hi):
        keys = [c[...].astype(BF16) for c in ckv_refs[lo:hi]]
        return keys, [_dot_nt(q_lat, kc) + _dot(q_rope, kr[...].astype(BF16))
                      for kc, kr in zip(keys, kr_refs[lo:hi])]

    sub = SAMPLE_SUBGROUP if n_pages_step % SAMPLE_SUBGROUP == 0 else n_pages_step
    state = (m_ref[...], l_ref[...], acc_ref[...])
    ahead = scores_of(0, sub)
    for lo in range(0, n_pages_step, sub):
        keys, scores = ahead
        if lo + sub < n_pages_step:
            ahead = scores_of(lo + sub, lo + 2 * sub)
        state = update(state, scores, keys)
    m_ref[...], l_ref[...], acc_ref[...] = state

    @pl.when(g == pl.num_programs(1) - 1)
    def _():
        kn = kn_ref[0]
        s_new = _dot_nt(q, kn).reshape(n_heads, ds, ds)
        causal = (lax.broadcasted_iota(jnp.int32, (ds, ds), 1) <= lax.broadcasted_iota(jnp.int32, (ds, ds), 0))
        s_new = jnp.where(causal[None], s_new, NEG).reshape(rows, ds)
        _, l_fin, acc_fin = update(state, [s_new], [kn[:, :KV_LORA]])
        o_lat = (acc_fin / _lane_tiles(l_fin, KV_LORA)).reshape(n_heads, ds, KV_LORA)

        def store(j, val):
            o_ref[0, :, j * LANES:(j + 1) * LANES] = val.astype(o_ref.dtype)
        _value_up(o_lat, wuv_ref, store)


def attention_sample(qf, kf_new, cache_ckv, cache_krope_t, layer, page_table, w_uv_pairs):
    db, n_heads, ds, _ = qf.shape
    n_pages = page_table.shape[1]
    gp = _pick(n_pages, PAGES_PER_STEP, 1)

    grid_spec = pltpu.PrefetchScalarGridSpec(
        num_scalar_prefetch=1,
        grid=(db, n_pages // gp),
        in_specs=[pl.BlockSpec((1, n_heads, ds, LATENT_PAD), lambda b, g, pt: (b, 0, 0, 0)),
                  pl.BlockSpec((1, ds, LATENT_PAD), lambda b, g, pt: (b, 0, 0)),
                  pl.BlockSpec(w_uv_pairs.shape, lambda b, g, pt: (0, 0, 0)),
                  pl.BlockSpec(memory_space=pl.ANY),
                  pl.BlockSpec(memory_space=pl.ANY)],
        out_specs=pl.BlockSpec((1, ds, n_heads * V_HEAD), lambda b, g, pt: (b, 0, 0)),
        scratch_shapes=[pltpu.VMEM((2, gp) + cache_ckv.shape[2:], cache_ckv.dtype),
                        pltpu.VMEM((2, gp) + cache_krope_t.shape[2:], cache_krope_t.dtype),
                        pltpu.SemaphoreType.DMA((2,)),
                        pltpu.VMEM((n_heads * ds, LANES), F32), pltpu.VMEM((n_heads * ds, LANES), F32),
                        pltpu.VMEM((n_heads * ds, KV_LORA), F32)],
    )
    return pl.pallas_call(
        functools.partial(_attn_sample_kernel, n_pages_step=gp, layer=layer),
        grid_spec=grid_spec,
        out_shape=jax.ShapeDtypeStruct((db, ds, n_heads * V_HEAD), BF16),
        compiler_params=_params("arbitrary", "arbitrary"),
        name="attention_sample",
    )(page_table, qf, kf_new, w_uv_pairs, cache_ckv, cache_krope_t)


def _merge_kernel(oa_ref, ob_ref, ga_ref, gb_ref, wa_ref, wb_ref, o_ref):
    ua = _dot(oa_ref[...], wa_ref[...])
    ub = _dot(ob_ref[...], wb_ref[...])
    o_ref[...] = (_sigmoid(ga_ref[...]) * ua + _sigmoid(gb_ref[...]) * ub).astype(o_ref.dtype)


def merge(o_a, o_b, p, ga_off, gb_off, w_up_a, w_up_b):
    n, d_a = o_a.shape
    d = w_up_a.shape[1]
    tm = _pick(n, 1376)
    tn = SEG
    return pl.pallas_call(
        _merge_kernel,
        grid=(n // tm, d // tn),
        in_specs=[pl.BlockSpec((tm, d_a), lambda i, j: (i, 0)),
                  pl.BlockSpec((tm, o_b.shape[1]), lambda i, j: (i, 0)),
                  pl.BlockSpec((tm, tn), lambda i, j: (i, ga_off // tn + j)),
                  pl.BlockSpec((tm, tn), lambda i, j: (i, gb_off // tn + j)),
                  pl.BlockSpec((d_a, tn), lambda i, j: (0, j)),
                  pl.BlockSpec((w_up_b.shape[0], tn), lambda i, j: (0, j))],
        out_specs=pl.BlockSpec((tm, tn), lambda i, j: (i, j)),
        out_shape=jax.ShapeDtypeStruct((n, d), BF16),
        compiler_params=_params("parallel", "arbitrary"),
        name="merge",
    )(o_a, o_b, p, p, w_up_a, w_up_b)


def _wo_kernel(m_ref, x_ref, w_ref, o_ref):
    o_ref[...] = x_ref[...] + _dot(m_ref[...], w_ref[...])


def out_proj(m, x, w_o):
    n, d = x.shape
    tm = _pick(n, 1376)
    tn = SEG
    return pl.pallas_call(
        _wo_kernel,
        grid=(n // tm, d // tn),
        in_specs=[pl.BlockSpec((tm, d), lambda i, j: (i, 0)),
                  pl.BlockSpec((tm, tn), lambda i, j: (i, j)),
                  pl.BlockSpec((d, tn), lambda i, j: (0, j))],
        out_specs=pl.BlockSpec((tm, tn), lambda i, j: (i, j)),
        out_shape=jax.ShapeDtypeStruct((n, d), F32),
        compiler_params=_params("parallel", "arbitrary"),
        name="out_proj",
    )(m, x, w_o)


def _router_kernel(x_ref, g_ref, w_ref, bias_ref, h_ref, r_ref):
    x = x_ref[...]
    h = x * lax.rsqrt(jnp.mean(x * x, -1, keepdims=True) + NORM_EPS) * g_ref[...]
    h_ref[...] = h.astype(h_ref.dtype)
    logits = _dot(h, w_ref[...], precision=HIGHEST)
    biased = logits + bias_ref[...]
    lane_i = lax.broadcasted_iota(jnp.int32, logits.shape, 1)
    lane = lane_i.astype(F32)
    lane_group = (lane_i // EXPERTS_PER_GROUP).astype(F32)
    is_g = (lane_i >= N_EXPERTS) & (lane_i < N_EXPERTS + N_GROUPS)
    big = float(1 << 20)

    def first_argmax(vals):
        mx = jnp.max(vals, axis=-1, keepdims=True)
        return jnp.min(jnp.where(vals == mx, lane, big), axis=-1, keepdims=True)

    def pick(vals, idx):
        return jnp.sum(jnp.where(lane == idx, vals, 0.0), axis=-1, keepdims=True)

    g_lane = first_argmax(jnp.where(is_g, biased, -jnp.inf))
    g_sel = g_lane - float(N_EXPERTS)
    g_max = jnp.max(jnp.where(is_g, logits, -jnp.inf), axis=-1, keepdims=True)
    g_exp = jnp.where(is_g, jnp.exp(logits - g_max), 0.0)
    p_sel = pick(g_exp, g_lane) / jnp.sum(g_exp, axis=-1, keepdims=True)

    in_grp = (lane_i < N_EXPERTS) & (lane_group == g_sel)
    e_biased = jnp.where(in_grp, biased, -jnp.inf)
    i1 = first_argmax(e_biased)
    i2 = first_argmax(jnp.where(lane == i1, -jnp.inf, e_biased))
    l1, l2 = pick(logits, i1), pick(logits, i2)
    mx = jnp.maximum(l1, l2)
    e1, e2 = jnp.exp(l1 - mx), jnp.exp(l2 - mx)
    w1 = e1 / (e1 + e2) * p_sel
    w2 = e2 / (e1 + e2) * p_sel
    r_ref[...] = jnp.where(lane_i == 0, i1, jnp.where(lane_i == 1, i2, jnp.where(lane_i == 2, w1,
                                                                                   jnp.where(lane_i == 3, w2, 0.0))))


def router(x, g, w_router, bias_router):
    n, d = x.shape
    tm = _pick(n, 688)
    return pl.pallas_call(
        _router_kernel,
        grid=(n // tm,),
        in_specs=[pl.BlockSpec((tm, d), lambda i: (i, 0)),
                  pl.BlockSpec((1, d), lambda i: (0, 0)),
                  pl.BlockSpec((d, LANES), lambda i: (0, 0)),
                  pl.BlockSpec((1, LANES), lambda i: (0, 0))],
        out_specs=[pl.BlockSpec((tm, d), lambda i: (i, 0)), pl.BlockSpec((tm, LANES), lambda i: (i, 0))],
        out_shape=[jax.ShapeDtypeStruct((n, d), BF16), jax.ShapeDtypeStruct((n, LANES), F32)],
        compiler_params=_params("parallel"),
        name="router",
    )(x, g, w_router, bias_router)


def _expert_kernel(te_ref, nt_ref, x_ref, gate_ref, wg_ref, wu_ref, wd_ref, o_ref, wg_bf, wu_bf, wd_bf):
    i = pl.program_id(0)

    @pl.when((i == 0) | (te_ref[i] != te_ref[jnp.maximum(i - 1, 0)]))
    def _():
        wg_bf[...] = wg_ref[...].astype(BF16)
        wu_bf[...] = wu_ref[...].astype(BF16)
        wd_bf[...] = wd_ref[...].astype(BF16)

    @pl.when(i < nt_ref[0])
    def _():
        x = x_ref[...]
        hg = _dot(x, wg_bf[...])
        hu = _dot(x, wu_bf[...])
        act = hg * _sigmoid(hg) * hu * gate_ref[...]
        o_ref[...] = _dot(act.astype(BF16), wd_bf[...])

    @pl.when(pl.program_id(0) >= nt_ref[0])
    def _():
        o_ref[...] = jnp.zeros(o_ref.shape, o_ref.dtype)


def experts(tile_expert, n_tiles_used, xg, gates, w_gate, w_up, w_down, layer):
    m_pad, d = xg.shape
    ff = w_gate.shape[-1]
    tm = EXPERT_TILE
    grid_spec = pltpu.PrefetchScalarGridSpec(
        num_scalar_prefetch=2,
        grid=(m_pad // tm,),
        in_specs=[pl.BlockSpec((tm, d), lambda i, te, nt: (i, 0)),
                  pl.BlockSpec((tm, 1), lambda i, te, nt: (i, 0)),
                  pl.BlockSpec((None, None, d, ff), lambda i, te, nt: (layer, te[i], 0, 0)),
                  pl.BlockSpec((None, None, d, ff), lambda i, te, nt: (layer, te[i], 0, 0)),
                  pl.BlockSpec((None, None, ff, d), lambda i, te, nt: (layer, te[i], 0, 0))],
        out_specs=pl.BlockSpec((tm, d), lambda i, te, nt: (i, 0)),
        scratch_shapes=[pltpu.VMEM((d, ff), BF16), pltpu.VMEM((d, ff), BF16), pltpu.VMEM((ff, d), BF16)],
    )
    return pl.pallas_call(
        _expert_kernel,
        grid_spec=grid_spec,
        out_shape=jax.ShapeDtypeStruct((m_pad, d), F32),
        compiler_params=_params("arbitrary"),
        name="experts",
    )(tile_expert, n_tiles_used, xg, gates, w_gate, w_up, w_down)


def _combine_kernel(x_ref, e0_ref, e1_ref, g_ref, o_ref):
    x = x_ref[...] + (e0_ref[...] + e1_ref[...])
    o_ref[...] = x * lax.rsqrt(jnp.mean(x * x, -1, keepdims=True) + NORM_EPS) * g_ref[...]


def combine(x, e0, e1, g, row_start):
    n, d = x.shape
    tm = _pick(int(np.gcd(n, row_start)) if row_start else n, 688)
    off = row_start // tm
    row = pl.BlockSpec((tm, d), lambda i: (i, 0))
    shifted = pl.BlockSpec((tm, d), lambda i: (i + off, 0))
    return pl.pallas_call(
        _combine_kernel,
        grid=(n // tm,),
        in_specs=[row, shifted, shifted, pl.BlockSpec((1, d), lambda i: (0, 0))],
        out_specs=row,
        out_shape=jax.ShapeDtypeStruct((n, d), F32),
        compiler_params=_params("parallel"),
        name="combine",
    )(x, e0, e1, g)


def _pad_cols(w, width):
    return jnp.pad(w, ((0, 0), (0, width - w.shape[1])))


def _pad_rows(w, height):
    return jnp.pad(w, ((0, height - w.shape[0]), (0, 0)))


def _swap_halves(w):
    half = w.shape[-1] // 2
    return jnp.concatenate([w[..., half:], w[..., :half]], axis=-1)


def _block_diag_pairs(w):
    h, a, b = w.shape
    z = jnp.zeros((h // 2, a, b), w.dtype)
    top = jnp.concatenate([w[0::2], z], axis=-1)
    bot = jnp.concatenate([z, w[1::2]], axis=-1)
    return jnp.concatenate([top, bot], axis=1)


def _rope_tables(pos):
    half = QK_ROPE // 2
    inv = ROPE_THETA ** (-jnp.arange(half, dtype=F32) / half)
    ang = pos.astype(F32)[:, None] * inv[None, :]
    cos, sin = jnp.cos(ang), jnp.sin(ang)
    cos_t = _pad_cols(jnp.concatenate([cos, cos], -1), LANES)
    sin_t = _pad_cols(jnp.concatenate([-sin, sin], -1), LANES)
    return cos_t, sin_t


def _layer_weights(l, w_in, rwkv_mu, rwkv_w0, rwkv_w2, rwkv_a0, rwkv_a2, rwkv_g2, rwkv_k_k, rwkv_k_a, rwkv_r_k,
                   rwkv_ln_g, rwkv_ln_b, mla_q_norm_g, mla_w_uq, mla_kv_norm_g, mla_w_uk, mla_w_uv, w_up_a,
                   w_up_b, w_o, norm_mix_g):
    d = w_in.shape[1]
    d_a = rwkv_w0.shape[1]
    n_heads = mla_w_uk.shape[2]
    off_kv, off_kr = Q_LORA, Q_LORA + KV_LORA
    off_rw = off_kr + QK_ROPE
    off_ga = off_rw + 3 * d_a + DECAY_LORA + AAA_LORA + GATE_LORA
    off_gb = off_ga + d
    w = w_in[l]
    rw = w[:, off_rw:off_ga]

    def lora_layout(m):
        wl = m[:, 3 * d_a:3 * d_a + DECAY_LORA]
        al = m[:, 3 * d_a + DECAY_LORA:3 * d_a + DECAY_LORA + AAA_LORA]
        gl = m[:, 3 * d_a + DECAY_LORA + AAA_LORA:]
        return jnp.concatenate([_pad_cols(wl, LANES), _pad_cols(al, LANES), gl], axis=1)

    kr = w[:, off_kr:off_rw]
    kv_seg = jnp.concatenate([w[:, off_kv:off_kr], _pad_cols(kr, LANES), _pad_cols(_swap_halves(kr), LANES)], axis=1)
    w_all = jnp.concatenate([rw[:, :3 * d_a], lora_layout(rw), w[:, off_ga:off_gb], w[:, off_gb:],
                             w[:, :Q_LORA], kv_seg], axis=1).astype(BF16)
    mu = rwkv_mu[l][None]
    mu_all = jnp.concatenate([mu[:, :3 * d_a], lora_layout(mu)], axis=1)

    uq = mla_w_uq[l].reshape(Q_LORA, n_heads, QK_NOPE + QK_ROPE)
    w_nope = uq[:, :, :QK_NOPE].reshape(Q_LORA, n_heads * QK_NOPE).astype(BF16)
    rope_cols = uq[:, :, QK_NOPE:]
    pad_heads = lambda m: jnp.pad(m, ((0, 0), (0, 0), (0, LANES - QK_ROPE))).reshape(Q_LORA, n_heads * LANES)
    w_rope = pad_heads(rope_cols).astype(BF16)
    w_rope_sw = pad_heads(_swap_halves(rope_cols)).astype(BF16)
    w_uk_pairs = _block_diag_pairs(jnp.transpose(mla_w_uk[l], (1, 2, 0))).astype(BF16)
    w_uv_pairs = _block_diag_pairs(jnp.transpose(mla_w_uv[l], (1, 0, 2))).astype(BF16)

    return dict(
        w_all=w_all, mu_all=mu_all, norm_mix_g=norm_mix_g[l][None],
        ga_off=3 * d_a + SEG, gb_off=3 * d_a + SEG + d,
        w0=rwkv_w0[l][None], a0=rwkv_a0[l][None], k_k=rwkv_k_k[l][None], k_a=rwkv_k_a[l][None],
        w2=_pad_rows(rwkv_w2[l], LANES).astype(BF16), a2=_pad_rows(rwkv_a2[l], LANES).astype(BF16),
        g2=rwkv_g2[l].astype(BF16),
        r_k=rwkv_r_k[l].reshape(1, d_a), ln_g=rwkv_ln_g[l][None], ln_b=rwkv_ln_b[l][None],
        g_q=mla_q_norm_g[l][None], g_kv=mla_kv_norm_g[l][None],
        w_nope=w_nope, w_rope=w_rope, w_rope_sw=w_rope_sw, w_uk_pairs=w_uk_pairs, w_uv_pairs=w_uv_pairs,
        w_up_a=w_up_a[l].astype(BF16), w_up_b=w_up_b[l].astype(BF16), w_o=w_o[l].astype(BF16),
        n_heads=n_heads, d_a=d_a, lora_layout=lora_layout,
    )


def _token_mixer(x3, pos, shift_prev, wkv_prev, lw, attend):
    b, t, d = x3.shape
    n = b * t
    d_a = lw["d_a"]
    x2 = x3.reshape(n, d)
    p = rms_matmul(x2, lw["norm_mix_g"], lw["w_all"])
    p3 = p.reshape(b, t, p.shape[1])

    chunk = _pick(t, SCAN_CHUNK)
    if t >= 344:
        bb, tt = 1, _pick(t, 344)
        tt_prep = _pick(t, 688, chunk)
    else:
        bb, tt = _pick(b, max(1, 256 // t), 1), t
        tt_prep = t

    def column_tiles(m):
        n_ct = d_a // SEG
        parts = [m[:, part * d_a:(part + 1) * d_a].reshape(-1, n_ct, SEG) for part in range(3)]
        lora = jnp.broadcast_to(m[:, None, 3 * d_a:], (m.shape[0], n_ct, SEG))
        return jnp.concatenate(parts + [lora], axis=-1)[:, :, None, :]

    shift_cols = jnp.concatenate([shift_prev[:, :3 * d_a], lw["lora_layout"](shift_prev)], axis=1)
    rt, qu, kh, bh, kt, bt, v, gam, bonus, g = rwkv_prepare(
        p3, column_tiles(shift_cols), column_tiles(lw["mu_all"])[0], lw["w0"], lw["a0"], lw["k_k"], lw["k_a"],
        lw["r_k"], lw["w2"], lw["a2"], lw["g2"], bb, tt_prep, chunk)
    y, s_new = rwkv_scan(rt, qu, kh, bh, kt, bt, v, gam, wkv_prev, chunk)
    flat = lambda a: a.reshape(n, d_a)
    o_a = rwkv_out(flat(y), flat(bonus), flat(g), lw["ln_g"], lw["ln_b"])

    cos_t, sin_t = _rope_tables(pos)
    qf, kf, ckv, krope = mla_project(p3, cos_t, sin_t, lw["g_q"], lw["g_kv"], lw["w_nope"], lw["w_rope"],
                                     lw["w_rope_sw"], lw["w_uk_pairs"], lw["n_heads"], bb, tt)
    o_b = attend(qf, kf).reshape(n, -1)

    m = merge(o_a, o_b, p, lw["ga_off"], lw["gb_off"], lw["w_up_a"], lw["w_up_b"])
    x_new = out_proj(m, x2, lw["w_o"])

    last = p3[:, -1, :]
    shift_new = jnp.concatenate([last[:, :3 * d_a],
                                 last[:, 3 * d_a:3 * d_a + DECAY_LORA],
                                 last[:, 3 * d_a + LANES:3 * d_a + LANES + AAA_LORA],
                                 last[:, 3 * d_a + 2 * LANES:3 * d_a + SEG]], axis=1)
    return x_new, (ckv, krope, s_new, shift_new)


def _moe(x_rows, norm_g, router_group_w, router_group_b, router_expert_w, router_expert_b,
         w_gate, w_up, w_down, layer, final_g):
    w_router = _pad_cols(jnp.concatenate([router_expert_w, router_group_w], axis=1), LANES)
    b_router = _pad_cols(jnp.concatenate([router_expert_b, router_group_b])[None], LANES)
    hs, routes = zip(*[router(x, norm_g, w_router, b_router) for x in x_rows])
    h = jnp.concatenate(hs, axis=0)
    route = jnp.concatenate(routes, axis=0)
    n = h.shape[0]
    eid = route[:, :2].astype(jnp.int32).reshape(-1)
    gate = route[:, 2:4].reshape(-1)

    tm = EXPERT_TILE
    n_tiles = -(-(2 * n + N_EXPERTS * (tm - 1)) // tm)
    m_pad = n_tiles * tm
    experts_iota = jnp.arange(N_EXPERTS, dtype=jnp.int32)
    order = jnp.argsort(eid, stable=True).astype(jnp.int32)
    rank = jnp.argsort(order).astype(jnp.int32)
    counts = jnp.sum((eid[:, None] == experts_iota[None, :]).astype(jnp.int32), axis=0)
    padded = ((counts + tm - 1) // tm) * tm
    seg_end = jnp.cumsum(padded)
    seg_start = seg_end - padded
    cnt_start = jnp.cumsum(counts) - counts
    dest = jnp.take(seg_start - cnt_start, eid, mode="clip") + rank
    tile_start = jnp.arange(n_tiles, dtype=jnp.int32) * tm
    tile_expert = jnp.minimum(jnp.sum((seg_end[None, :] <= tile_start[:, None]).astype(jnp.int32), axis=1),
                              N_EXPERTS - 1)
    n_used = (seg_end[-1] // tm).astype(jnp.int32).reshape(1)
    row_expert = jnp.repeat(tile_expert, tm)
    within = jnp.arange(m_pad, dtype=jnp.int32) - jnp.take(seg_start, row_expert, mode="clip")
    valid = within < jnp.take(counts, row_expert, mode="clip")
    row_assign = jnp.take(order, jnp.take(cnt_start, row_expert, mode="clip") + within, mode="clip")
    row_token = jnp.where(valid, row_assign // 2, 0)
    row_gate = jnp.where(valid, jnp.take(gate, row_assign, mode="clip"), 0.0)

    xg = jnp.take(h, row_token, axis=0, mode="clip")
    eo = experts(tile_expert, n_used, xg, row_gate[:, None], w_gate, w_up, w_down, layer)
    dest2 = dest.reshape(n, 2)
    e0 = jnp.take(eo, dest2[:, 0], axis=0, mode="clip")
    e1 = jnp.take(eo, dest2[:, 1], axis=0, mode="clip")

    outs, start = [], 0
    for x in x_rows:
        outs.append(combine(x, e0, e1, final_g, start))
        start += x.shape[0]
    return outs


def kernel(x_prompt, x_sample, cache_ckv, cache_krope, state_wkv, state_shift, page_table, meta_tokens, norm_mix_g, w_in, rwkv_mu, rwkv_w0, rwkv_w2, rwkv_a0, rwkv_a2, rwkv_g2, rwkv_k_k, rwkv_k_a, rwkv_r_k, rwkv_ln_g, rwkv_ln_b, mla_q_norm_g, mla_w_uq, mla_kv_norm_g, mla_w_uk, mla_w_uv, w_up_a, w_up_b, w_o, norm_ffn_g, router_group_w, router_group_b, router_expert_w, router_expert_b, expert_w_gate, expert_w_up, expert_w_down, norm_final_g):
    depth = w_in.shape[0]
    assert depth == 1, "the final norm is fused into the MoE combine, which assumes a single layer"
    b, s, d = x_prompt.shape
    db, ds, _ = x_sample.shape
    past = page_table.shape[1] * PAGE_SIZE
    t = N_META + s
    n_heads_a = state_wkv.shape[2]
    rwkv_cols = state_shift.shape[-1]

    xp = jnp.concatenate([jnp.broadcast_to(meta_tokens[None], (b, N_META, d)), x_prompt], axis=1)
    l = 0
    lw = _layer_weights(l, w_in, rwkv_mu, rwkv_w0, rwkv_w2, rwkv_a0, rwkv_a2, rwkv_g2, rwkv_k_k, rwkv_k_a,
                        rwkv_r_k, rwkv_ln_g, rwkv_ln_b, mla_q_norm_g, mla_w_uq, mla_kv_norm_g, mla_w_uk,
                        mla_w_uv, w_up_a, w_up_b, w_o, norm_mix_g)

    xp1, st_p = _token_mixer(xp, jnp.arange(t), jnp.zeros((b, rwkv_cols), F32),
                             jnp.zeros((b, n_heads_a, HEAD_A, HEAD_A), F32), lw,
                             lambda qf, kf: attention_prompt(qf, kf, lw["w_uv_pairs"]))
    xs1, st_s = _token_mixer(x_sample, past + jnp.arange(ds), state_shift[l], state_wkv[l], lw,
                             lambda qf, kf: attention_sample(qf, kf, cache_ckv, jnp.swapaxes(cache_krope, 2, 3), l,
                                                             page_table, lw["w_uv_pairs"]))

    yp, ys = _moe([xp1, xs1], norm_ffn_g[l][None], router_group_w[l], router_group_b[l], router_expert_w[l],
                  router_expert_b[l], expert_w_gate, expert_w_up, expert_w_down, l, norm_final_g[None])
    y_prompt = yp.reshape(b, t, d)[:, N_META:]
    y_sample = ys.reshape(db, ds, d)
    return (y_prompt, y_sample,
            st_p[0][None], st_p[1][None], st_p[2][None], st_p[3][None],
            st_s[0][None], st_s[1][None], st_s[2][None], st_s[3][None])
```

```python
import functools

import jax
import jax.numpy as jnp
import numpy as np
from jax import lax
from jax.experimental import pallas as pl
from jax.experimental.pallas import tpu as pltpu

F32 = jnp.float32
BF16 = jnp.bfloat16
HIGHEST = lax.Precision.HIGHEST

N_META = 16
NORM_EPS = 1e-6
NEG = -1e30
HEAD_A = 64
DECAY_LORA = 96
AAA_LORA = 96
GATE_LORA = 256
LNX_EPS = 64e-5
QK_NOPE = 64
QK_ROPE = 32
V_HEAD = 64
Q_LORA = 512
KV_LORA = 256
ROPE_THETA = 10000.0
PAGE_SIZE = 128
N_GROUPS = 4
EXPERTS_PER_GROUP = 8
N_EXPERTS = N_GROUPS * EXPERTS_PER_GROUP

LANES = 128
HEAD_PAIR = 2 * HEAD_A
SEG = 512
LATENT_PAD = KV_LORA + LANES
VMEM_LIMIT = 56 * 1024 * 1024
SCAN_CHUNK = 16
SCAN_BATCH = 8
EXPERT_TILE = 256
PAGES_PER_STEP = 64
SAMPLE_SUBGROUP = 8
ATTN_HEAD_GROUP = 2


def _pick(n, target, mult=8):
    best = None
    for d in range(mult, min(n, target) + 1, mult):
        if n % d == 0:
            best = d
    return best if best is not None else n


def _params(*sem):
    return pltpu.CompilerParams(dimension_semantics=sem, vmem_limit_bytes=VMEM_LIMIT)


def _dot(a, b, **kw):
    return jnp.dot(a, b, preferred_element_type=F32, **kw)


def _dot_nt(a, b):
    return lax.dot_general(a, b, (((1,), (1,)), ((), ())), preferred_element_type=F32)


def _dot_tn(a, b):
    return lax.dot_general(a, b, (((0,), (0,)), ((), ())), preferred_element_type=F32)


def _sigmoid(x):
    return 1.0 / (1.0 + jnp.exp(-x))


def _head_ones(scale=1.0):
    r = lax.broadcasted_iota(jnp.int32, (LANES, LANES), 0) // HEAD_A
    c = lax.broadcasted_iota(jnp.int32, (LANES, LANES), 1) // HEAD_A
    return jnp.where(r == c, scale, 0.0).astype(F32)


def _rms_mm_kernel(x_ref, g_ref, w_ref, o_ref, h_ref):
    @pl.when(pl.program_id(1) == 0)
    def _():
        x = x_ref[...]
        ms = jnp.mean(x * x, axis=-1, keepdims=True)
        h_ref[...] = (x * lax.rsqrt(ms + NORM_EPS) * g_ref[...]).astype(BF16)

    o_ref[...] = _dot(h_ref[...], w_ref[...])


def rms_matmul(x, g, w):
    n, k = x.shape
    m = w.shape[1]
    tm = _pick(n, 1376)
    tn = SEG
    return pl.pallas_call(
        _rms_mm_kernel,
        grid=(n // tm, m // tn),
        in_specs=[pl.BlockSpec((tm, k), lambda i, j: (i, 0)),
                  pl.BlockSpec((1, k), lambda i, j: (0, 0)),
                  pl.BlockSpec((k, tn), lambda i, j: (0, j))],
        out_specs=pl.BlockSpec((tm, tn), lambda i, j: (i, j)),
        out_shape=jax.ShapeDtypeStruct((n, m), F32),
        scratch_shapes=[pltpu.VMEM((tm, k), BF16)],
        compiler_params=_params("parallel", "arbitrary"),
        name="rms_matmul",
    )(x, g, w)


def _prepare_kernel(pr_ref, pk_ref, pv_ref, pl_ref, sh_ref, mu_ref, w0_ref, a0_ref, kk_ref, ka_ref, rk_ref,
                    w2_ref, a2_ref, g2_ref,
                    rt_out, qu_out, kh_out, bh_out, kt_out, bt_out, v_out, gam_out, bonus_out, g_out,
                    carry_ref, *, chunk):
    tc = pr_ref.shape[-1]
    ct = pl.program_id(2)

    @pl.when(pl.program_id(1) == 0)
    def _():
        carry_ref[ct] = sh_ref[...]

    def shifted(ref, lo, width):
        x = ref[...]
        prev_row = carry_ref[ct, :, :, lo:lo + width]
        t_idx = lax.broadcasted_iota(jnp.int32, x.shape, 1)
        prev = jnp.where(t_idx == 0, prev_row, pltpu.roll(x, 1, axis=1))
        carry_ref[ct, :, :, lo:lo + width] = x[:, x.shape[1] - 1:, :]
        xs = x + (prev - x) * mu_ref[:, lo:lo + width]
        return xs.reshape(x.shape[0] * x.shape[1], width)

    bb, tt, _ = pr_ref.shape
    rows = bb * tt
    r = shifted(pr_ref, 0, tc)
    k = shifted(pk_ref, tc, tc)
    v = shifted(pv_ref, 2 * tc, tc)
    lora = shifted(pl_ref, 3 * tc, SEG)
    wl, al, gl = lora[:, :LANES], lora[:, LANES:2 * LANES], lora[:, 2 * LANES:]

    z = -(w0_ref[...] + _dot(jnp.tanh(wl).astype(BF16), w2_ref[...]))
    softplus = jnp.maximum(z, 0.0) + jnp.log(1.0 + jnp.exp(-jnp.abs(z)))
    lw = -jnp.exp(-softplus - 0.5)
    a = _sigmoid(a0_ref[...] + _dot(al.astype(BF16), a2_ref[...]))
    g = _dot(_sigmoid(gl).astype(BF16), g2_ref[...])

    kkf = k * kk_ref[...]
    ones = _head_ones()
    per_head = lambda x: jnp.concatenate(
        [_dot(x[:, j * LANES:(j + 1) * LANES], ones, precision=HIGHEST) for j in range(tc // LANES)], axis=-1)
    kk = kkf * lax.rsqrt(jnp.maximum(per_head(jnp.square(kkf)), 1e-24))
    k_mod = k * (1.0 + (a - 1.0) * ka_ref[...])
    b = kk * a
    bonus = per_head(r * k_mod * rk_ref[...]) * v

    pos = lax.broadcasted_iota(jnp.int32, (rows, tc), 0) % chunk
    cum = lw
    sh = 1
    while sh < chunk:
        cum = cum + jnp.where(pos >= sh, pltpu.roll(cum, sh, axis=0), 0.0)
        sh *= 2
    total = jnp.where(pos == chunk - 1, cum, 0.0)
    sh = 1
    while sh < chunk:
        total = total + pltpu.roll(total, rows - sh, axis=0)
        sh *= 2
    e_neg = jnp.exp(-cum)
    e_tail = jnp.exp(total - cum)

    def st(ref, val):
        ref[...] = val.reshape(bb, tt, tc).astype(ref.dtype)

    st(rt_out, r * jnp.exp(cum))
    st(qu_out, kk * jnp.exp(cum - lw))
    st(kh_out, k_mod * e_neg)
    st(bh_out, b * e_neg)
    st(kt_out, k_mod * e_tail)
    st(bt_out, b * e_tail)
    st(v_out, v)
    st(bonus_out, bonus)
    st(g_out, g)
    gam_out[...] = jnp.exp(total).reshape(bb, tt // chunk, chunk, tc)[:, :, 0:1, :]


def rwkv_prepare(p3, shift4, mu3, w0, a0, k_k, k_a, r_k, w2, a2, g2, bb, tt, chunk):
    b, t, _ = p3.shape
    d_a = w0.shape[-1]
    tc = SEG
    n_ct = d_a // tc
    width = 3 * tc + SEG
    row = lambda part: pl.BlockSpec((bb, tt, tc), lambda i, j, c, part=part: (i, j, part * n_ct + c))
    vec = pl.BlockSpec((1, tc), lambda i, j, c: (0, c))
    mat = lambda a: pl.BlockSpec((a.shape[0], tc), lambda i, j, c: (0, c))
    out_spec = pl.BlockSpec((bb, tt, tc), lambda i, j, c: (i, j, c))
    seq = lambda dt: jax.ShapeDtypeStruct((b, t, d_a), dt)
    return pl.pallas_call(
        functools.partial(_prepare_kernel, chunk=chunk),
        grid=(b // bb, t // tt, n_ct),
        in_specs=[row(0), row(1), row(2),
                  pl.BlockSpec((bb, tt, SEG), lambda i, j, c: (i, j, 3 * d_a // SEG)),
                  pl.BlockSpec((bb, None, 1, width), lambda i, j, c: (i, c, 0, 0)),
                  pl.BlockSpec((None, 1, width), lambda i, j, c: (c, 0, 0)),
                  vec, vec, vec, vec, vec, mat(w2), mat(a2), mat(g2)],
        out_specs=[out_spec] * 7
                  + [pl.BlockSpec((bb, tt // chunk, 1, tc), lambda i, j, c: (i, j, 0, c)), out_spec, out_spec],
        out_shape=[seq(BF16)] * 7 + [jax.ShapeDtypeStruct((b, t // chunk, 1, d_a), F32), seq(F32), seq(F32)],
        scratch_shapes=[pltpu.VMEM((n_ct, bb, 1, width), F32)],
        compiler_params=_params("parallel", "arbitrary", "arbitrary"),
        name="rwkv_prepare",
    )(p3, p3, p3, p3, shift4, mu3, w0, a0, k_k, k_a, r_k, w2, a2, g2)


def _scan_kernel(rt_ref, qu_ref, kh_ref, bh_ref, kt_ref, bt_ref, v_ref, gam_ref, s0_ref, y_ref, sf_ref, st_ref,
                 *, chunk):
    c_idx = pl.program_id(1)
    nb, n_pairs = st_ref.shape[:2]
    chains = [(bi, j) for bi in range(nb) for j in range(n_pairs)]
    c2x = 2 * chunk

    @pl.when(c_idx == 0)
    def _():
        zero = jnp.zeros((HEAD_A, HEAD_A), F32)
        for bi, j in chains:
            top = jnp.concatenate([s0_ref[bi, 2 * j], zero], axis=1)
            bot = jnp.concatenate([zero, s0_ref[bi, 2 * j + 1]], axis=1)
            st_ref[bi, j] = jnp.concatenate([top, bot], axis=0)

    r2 = lax.broadcasted_iota(jnp.int32, (c2x, c2x), 0)
    c2 = lax.broadcasted_iota(jnp.int32, (c2x, c2x), 1)
    strict = c2 < r2
    incl = c2 <= r2
    head0 = lax.broadcasted_iota(jnp.int32, (chunk, LANES), 1) < HEAD_A

    def stack2(ref, bi, j):
        x = ref[bi, :, j * LANES:(j + 1) * LANES].astype(F32)
        return jnp.concatenate([jnp.where(head0, x, 0.0), jnp.where(head0, 0.0, x)], axis=0).astype(BF16)

    each = lambda f, *lists: [f(*args) for args in zip(*lists)]
    s_prev = [st_ref[bi, j] for bi, j in chains]
    gam = [gam_ref[bi, 0, :, j * LANES:(j + 1) * LANES] for bi, j in chains]
    q_u, q_r, k_h, b_h, k_t, b_t, v2 = [[stack2(ref, bi, j) for bi, j in chains]
                                        for ref in (qu_ref, rt_ref, kh_ref, bh_ref, kt_ref, bt_ref, v_ref)]
    qq = each(lambda a, b: jnp.concatenate([a, b], axis=0), q_u, q_r)
    s_bf = [s.astype(BF16) for s in s_prev]
    sk = each(_dot_nt, qq, k_h)
    sb = each(_dot_nt, qq, b_h)
    ss = each(_dot_nt, qq, s_bf)
    a_vk = [jnp.where(strict, m[:c2x], 0.0).astype(BF16) for m in sk]
    power = [jnp.where(strict, -m[:c2x], 0.0) for m in sb]
    a_rk = [jnp.where(incl, m[c2x:], 0.0).astype(BF16) for m in sk]
    a_rb = [jnp.where(incl, m[c2x:], 0.0).astype(BF16) for m in sb]
    x = each(lambda s, a, v: s[:c2x] + _dot(a, v), ss, a_vk, v2)
    steps = max(1, int(np.ceil(np.log2(chunk))))
    for s in range(steps):
        pb = [p.astype(BF16) for p in power]
        x = each(lambda xi, p: xi + _dot(p, xi.astype(BF16)), x, pb)
        if s + 1 < steps:
            power = each(_dot, pb, pb)
    u2 = [xi.astype(BF16) for xi in x]
    y2 = each(lambda s, ak, v, ab, u: s[c2x:] + _dot(ak, v) - _dot(ab, u), ss, a_rk, v2, a_rb, u2)
    s_new = each(lambda s, g, v, u, kt, bt: s * g + _dot_tn(jnp.concatenate([v, -u], axis=0),
                                                              jnp.concatenate([kt, bt], axis=0)),
                 s_prev, gam, v2, u2, k_t, b_t)

    for (bi, j), y, s in zip(chains, y2, s_new):
        y_ref[bi, :, j * LANES:(j + 1) * LANES] = y[:chunk] + y[chunk:]
        st_ref[bi, j] = s

    @pl.when(c_idx == pl.num_programs(1) - 1)
    def _():
        for (bi, j), s in zip(chains, s_new):
            sf_ref[bi, 2 * j] = s[:HEAD_A, :HEAD_A]
            sf_ref[bi, 2 * j + 1] = s[HEAD_A:, HEAD_A:]


def rwkv_scan(rt, qu, kh, bh, kt, bt, v, gam, s0, chunk):
    bsz, t, d_a = rt.shape
    n_pairs = d_a // LANES
    nb = SCAN_BATCH if bsz % SCAN_BATCH == 0 else 1
    seq = pl.BlockSpec((nb, chunk, d_a), lambda i, c: (i, c, 0))
    st = pl.BlockSpec((nb, 2 * n_pairs, HEAD_A, HEAD_A), lambda i, c: (i, 0, 0, 0))
    return pl.pallas_call(
        functools.partial(_scan_kernel, chunk=chunk),
        grid=(bsz // nb, t // chunk),
        in_specs=[seq] * 7 + [pl.BlockSpec((nb, 1, 1, d_a), lambda i, c: (i, c, 0, 0)), st],
        out_specs=[seq, st],
        out_shape=[jax.ShapeDtypeStruct((bsz, t, d_a), F32),
                   jax.ShapeDtypeStruct((bsz, 2 * n_pairs, HEAD_A, HEAD_A), F32)],
        scratch_shapes=[pltpu.VMEM((nb, n_pairs, LANES, LANES), F32)],
        compiler_params=_params("parallel", "arbitrary"),
        name="rwkv_scan",
    )(rt, qu, kh, bh, kt, bt, v, gam, s0)


def _rwkv_out_kernel(y_ref, bonus_ref, g_ref, lng_ref, lnb_ref, o_ref):
    mean_m = _head_ones(1.0 / HEAD_A)
    for j in range(y_ref.shape[-1] // LANES):
        sl = slice(j * LANES, (j + 1) * LANES)
        y = y_ref[:, sl]
        d = y - _dot(y, mean_m, precision=HIGHEST)
        var = _dot(d * d, mean_m, precision=HIGHEST)
        yn = d * lax.rsqrt(var + LNX_EPS) * lng_ref[:, sl] + lnb_ref[:, sl]
        o_ref[:, sl] = ((yn + bonus_ref[:, sl]) * g_ref[:, sl]).astype(o_ref.dtype)


def rwkv_out(y, bonus, g, ln_g, ln_b):
    n, d_a = y.shape
    tm = _pick(n, 688)
    row = pl.BlockSpec((tm, d_a), lambda i: (i, 0))
    vec = pl.BlockSpec((1, d_a), lambda i: (0, 0))
    return pl.pallas_call(
        _rwkv_out_kernel,
        grid=(n // tm,),
        in_specs=[row] * 3 + [vec] * 2,
        out_specs=row,
        out_shape=jax.ShapeDtypeStruct((n, d_a), BF16),
        compiler_params=_params("parallel"),
        name="rwkv_out",
    )(y, bonus, g, ln_g, ln_b)


def _mla_kernel(pq_ref, pkv_ref, cos_ref, sin_ref, gq_ref, gkv_ref, wn_ref, wr_ref, wrs_ref, wuk_ref,
                qf_ref, kf_ref, ckv_ref, kr_ref, *, scale):
    bb, tt, _ = pq_ref.shape
    rows = bb * tt
    n_heads = qf_ref.shape[1]
    cos = jnp.broadcast_to(cos_ref[...][None], (bb, tt, LANES)).reshape(rows, LANES)
    sin = jnp.broadcast_to(sin_ref[...][None], (bb, tt, LANES)).reshape(rows, LANES)

    pq = pq_ref[...].reshape(rows, Q_LORA)
    c_q = (pq * lax.rsqrt(jnp.mean(pq * pq, -1, keepdims=True) + NORM_EPS) * gq_ref[...]).astype(BF16)
    q_nope = _dot(c_q, wn_ref[...]).astype(BF16)
    q_rope = _dot(c_q, wr_ref[...])
    q_rope_sw = _dot(c_q, wrs_ref[...])
    for j in range(n_heads // 2):
        lat = _dot(q_nope[:, j * LANES:(j + 1) * LANES], wuk_ref[j]) * scale
        for hh in range(2):
            h = 2 * j + hh
            qf_ref[:, h, :, :KV_LORA] = lat[:, hh * KV_LORA:(hh + 1) * KV_LORA].reshape(bb, tt, KV_LORA).astype(BF16)
            sl = slice(h * LANES, (h + 1) * LANES)
            roped = (q_rope[:, sl] * cos + q_rope_sw[:, sl] * sin) * scale
            qf_ref[:, h, :, KV_LORA:] = roped.reshape(bb, tt, LANES).astype(BF16)

    pkv = pkv_ref[...].reshape(rows, SEG)
    kv = pkv[:, :KV_LORA]
    ckv = kv * lax.rsqrt(jnp.mean(kv * kv, -1, keepdims=True) + NORM_EPS) * gkv_ref[...]
    krope = pkv[:, KV_LORA:KV_LORA + LANES] * cos + pkv[:, KV_LORA + LANES:] * sin
    ckv_ref[...] = ckv.reshape(bb, tt, KV_LORA)
    kr_ref[...] = krope[:, :QK_ROPE].reshape(bb, tt, QK_ROPE)
    kf_ref[:, :, :KV_LORA] = ckv.reshape(bb, tt, KV_LORA).astype(BF16)
    kf_ref[:, :, KV_LORA:] = krope.reshape(bb, tt, LANES).astype(BF16)


def mla_project(p3, cos_t, sin_t, g_q, g_kv, w_nope, w_rope, w_rope_sw, w_uk_pairs, n_heads, bb, tt):
    b, t, width = p3.shape
    q_blk = (width - 2 * SEG) // SEG
    full = lambda a: pl.BlockSpec(a.shape, lambda i, j: (0,) * a.ndim)
    scale = float((QK_NOPE + QK_ROPE) ** -0.5)
    return pl.pallas_call(
        functools.partial(_mla_kernel, scale=scale),
        grid=(b // bb, t // tt),
        in_specs=[pl.BlockSpec((bb, tt, SEG), lambda i, j: (i, j, q_blk)),
                  pl.BlockSpec((bb, tt, SEG), lambda i, j: (i, j, q_blk + 1)),
                  pl.BlockSpec((tt, LANES), lambda i, j: (j, 0)),
                  pl.BlockSpec((tt, LANES), lambda i, j: (j, 0)),
                  full(g_q), full(g_kv), full(w_nope), full(w_rope), full(w_rope_sw), full(w_uk_pairs)],
        out_specs=[pl.BlockSpec((bb, n_heads, tt, LATENT_PAD), lambda i, j: (i, 0, j, 0)),
                   pl.BlockSpec((bb, tt, LATENT_PAD), lambda i, j: (i, j, 0)),
                   pl.BlockSpec((bb, tt, KV_LORA), lambda i, j: (i, j, 0)),
                   pl.BlockSpec((bb, tt, QK_ROPE), lambda i, j: (i, j, 0))],
        out_shape=[jax.ShapeDtypeStruct((b, n_heads, t, LATENT_PAD), BF16),
                   jax.ShapeDtypeStruct((b, t, LATENT_PAD), BF16),
                   jax.ShapeDtypeStruct((b, t, KV_LORA), F32),
                   jax.ShapeDtypeStruct((b, t, QK_ROPE), F32)],
        compiler_params=_params("parallel", "parallel"),
        name="mla_project",
    )(p3, p3, cos_t, sin_t, g_q, g_kv, w_nope, w_rope, w_rope_sw, w_uk_pairs)


def _lane_tiles(x, width):
    return jnp.concatenate([x] * (-(-width // LANES)), axis=1)[:, :width]


def _value_up(o_lat, wuv_ref, store):
    n_heads = o_lat.shape[0]
    for j in range(n_heads // 2):
        lhs = jnp.concatenate([o_lat[2 * j], o_lat[2 * j + 1]], axis=-1).astype(BF16)
        store(j, _dot(lhs, wuv_ref[j]))


def _attn_prompt_kernel(q_ref, k_ref, wuv_ref, o_ref, m_ref, l_ref, acc_ref, *, tq):
    qi, ki = pl.program_id(1), pl.program_id(2)
    n_heads = q_ref.shape[1]
    rows = n_heads * tq

    @pl.when(ki == 0)
    def _():
        m_ref[...] = jnp.full(m_ref.shape, NEG, F32)
        l_ref[...] = jnp.zeros(l_ref.shape, F32)
        acc_ref[...] = jnp.zeros(acc_ref.shape, F32)

    def step(diagonal):
        k = k_ref[0]
        hg = ATTN_HEAD_GROUP if n_heads % ATTN_HEAD_GROUP == 0 else n_heads
        grows = hg * tq

        def scores(g):
            return _dot_nt(q_ref[0, g * hg:(g + 1) * hg].reshape(grows, LATENT_PAD), k)

        def softmax_pv(g, s):
            sl = slice(g * grows, (g + 1) * grows)
            if diagonal:
                causal = (lax.broadcasted_iota(jnp.int32, (tq, tq), 1)
                          <= lax.broadcasted_iota(jnp.int32, (tq, tq), 0))
                s = jnp.where(causal[None], s.reshape(hg, tq, tq), NEG).reshape(grows, tq)
            m_prev = m_ref[sl]
            m_new = jnp.maximum(m_prev, jnp.max(s, axis=-1, keepdims=True))
            corr = jnp.exp(m_prev - m_new)
            p = jnp.exp(s - _lane_tiles(m_new, tq))
            l_ref[sl] = l_ref[sl] * corr + jnp.sum(p, axis=-1, keepdims=True)
            acc_ref[sl] = acc_ref[sl] * _lane_tiles(corr, KV_LORA) + _dot(p.astype(BF16), k[:, :KV_LORA])
            m_ref[sl] = m_new

        n_groups = n_heads // hg
        s_next = scores(0)
        for g in range(n_groups):
            s_cur = s_next
            if g + 1 < n_groups:
                s_next = scores(g + 1)
            softmax_pv(g, s_cur)

    @pl.when(ki < qi)
    def _():
        step(False)

    @pl.when(ki == qi)
    def _():
        step(True)
        o_lat = (acc_ref[...] / _lane_tiles(l_ref[...], KV_LORA)).reshape(n_heads, tq, KV_LORA)

        def store(j, val):
            o_ref[0, :, j * LANES:(j + 1) * LANES] = val.astype(o_ref.dtype)
        _value_up(o_lat, wuv_ref, store)


def attention_prompt(qf, kf, w_uv_pairs):
    b, n_heads, t, _ = qf.shape
    tq = _pick(t, 344)
    nq = t // tq
    return pl.pallas_call(
        functools.partial(_attn_prompt_kernel, tq=tq),
        grid=(b, nq, nq),
        in_specs=[pl.BlockSpec((1, n_heads, tq, LATENT_PAD), lambda i, q, k: (i, 0, q, 0)),
                  pl.BlockSpec((1, tq, LATENT_PAD), lambda i, q, k: (i, jnp.minimum(k, q), 0)),
                  pl.BlockSpec(w_uv_pairs.shape, lambda i, q, k: (0, 0, 0))],
        out_specs=pl.BlockSpec((1, tq, n_heads * V_HEAD), lambda i, q, k: (i, q, 0)),
        out_shape=jax.ShapeDtypeStruct((b, t, n_heads * V_HEAD), BF16),
        scratch_shapes=[pltpu.VMEM((n_heads * tq, LANES), F32), pltpu.VMEM((n_heads * tq, LANES), F32),
                        pltpu.VMEM((n_heads * tq, KV_LORA), F32)],
        compiler_params=_params("parallel", "parallel", "arbitrary"),
        name="attention_prompt",
    )(qf, kf, w_uv_pairs)


def _attn_sample_kernel(pt_ref, q_ref, kn_ref, wuv_ref, ckv_hbm, kr_hbm, o_ref, kbuf, rbuf, sem, m_ref, l_ref,
                        acc_ref, *, n_pages_step, layer):
    b, g = pl.program_id(0), pl.program_id(1)
    n_groups = pl.num_programs(1)
    step = b * n_groups + g
    slot = step % 2
    n_heads, ds = q_ref.shape[1], q_ref.shape[2]
    rows = n_heads * ds

    def page_copies(bb, gg, sl):
        copies = []
        for i in range(n_pages_step):
            page = pt_ref[bb, gg * n_pages_step + i]
            copies.append(pltpu.make_async_copy(ckv_hbm.at[layer, page], kbuf.at[sl, i], sem.at[sl]))
            copies.append(pltpu.make_async_copy(kr_hbm.at[layer, page], rbuf.at[sl, i], sem.at[sl]))
        return copies

    @pl.when(step == 0)
    def _():
        for c in page_copies(b, g, slot):
            c.start()

    @pl.when(step + 1 < pl.num_programs(0) * n_groups)
    def _():
        wrap = g + 1 == n_groups
        for c in page_copies(jnp.where(wrap, b + 1, b), jnp.where(wrap, 0, g + 1), 1 - slot):
            c.start()

    for c in page_copies(b, g, slot):
        c.wait()
    ckv_refs = [kbuf.at[slot, i] for i in range(n_pages_step)]
    kr_refs = [rbuf.at[slot, i] for i in range(n_pages_step)]

    @pl.when(g == 0)
    def _():
        m_ref[...] = jnp.full(m_ref.shape, NEG, F32)
        l_ref[...] = jnp.zeros(l_ref.shape, F32)
        acc_ref[...] = jnp.zeros(acc_ref.shape, F32)

    q = q_ref[0].reshape(rows, LATENT_PAD)
    q_lat, q_rope = q[:, :KV_LORA], q[:, KV_LORA:KV_LORA + QK_ROPE]

    def update(state, scores, values):
        m_prev, l_prev, acc = state
        s_max = functools.reduce(jnp.maximum, scores)
        m_new = jnp.maximum(m_prev, jnp.max(s_max, axis=-1, keepdims=True))
        corr = jnp.exp(m_prev - m_new)
        m_wide = _lane_tiles(m_new, scores[0].shape[1])
        probs = [jnp.exp(s - m_wide) for s in scores]
        p_sum = functools.reduce(jnp.add, probs)
        acc = acc * _lane_tiles(corr, KV_LORA)
        for p, val in zip(probs, values):
            acc = acc + _dot(p.astype(BF16), val)
        return m_new, l_prev * corr + jnp.sum(p_sum, axis=-1, keepdims=True), acc

    def scores_of(lo, hi):
        keys = [c[...].astype(BF16) for c in ckv_refs[lo:hi]]
        return keys, [_dot_nt(q_lat, kc) + _dot(q_rope, kr[...].astype(BF16))
                      for kc, kr in zip(keys, kr_refs[lo:hi])]

    sub = SAMPLE_SUBGROUP if n_pages_step % SAMPLE_SUBGROUP == 0 else n_pages_step
    state = (m_ref[...], l_ref[...], acc_ref[...])
    ahead = scores_of(0, sub)
    for lo in range(0, n_pages_step, sub):
        keys, scores = ahead
        if lo + sub < n_pages_step:
            ahead = scores_of(lo + sub, lo + 2 * sub)
        state = update(state, scores, keys)
    m_ref[...], l_ref[...], acc_ref[...] = state

    @pl.when(g == pl.num_programs(1) - 1)
    def _():
        kn = kn_ref[0]
        s_new = _dot_nt(q, kn).reshape(n_heads, ds, ds)
        causal = (lax.broadcasted_iota(jnp.int32, (ds, ds), 1) <= lax.broadcasted_iota(jnp.int32, (ds, ds), 0))
        s_new = jnp.where(causal[None], s_new, NEG).reshape(rows, ds)
        _, l_fin, acc_fin = update(state, [s_new], [kn[:, :KV_LORA]])
        o_lat = (acc_fin / _lane_tiles(l_fin, KV_LORA)).reshape(n_heads, ds, KV_LORA)

        def store(j, val):
            o_ref[0, :, j * LANES:(j + 1) * LANES] = val.astype(o_ref.dtype)
        _value_up(o_lat, wuv_ref, store)


def attention_sample(qf, kf_new, cache_ckv, cache_krope_t, layer, page_table, w_uv_pairs):
    db, n_heads, ds, _ = qf.shape
    n_pages = page_table.shape[1]
    gp = _pick(n_pages, PAGES_PER_STEP, 1)

    grid_spec = pltpu.PrefetchScalarGridSpec(
        num_scalar_prefetch=1,
        grid=(db, n_pages // gp),
        in_specs=[pl.BlockSpec((1, n_heads, ds, LATENT_PAD), lambda b, g, pt: (b, 0, 0, 0)),
                  pl.BlockSpec((1, ds, LATENT_PAD), lambda b, g, pt: (b, 0, 0)),
                  pl.BlockSpec(w_uv_pairs.shape, lambda b, g, pt: (0, 0, 0)),
                  pl.BlockSpec(memory_space=pl.ANY),
                  pl.BlockSpec(memory_space=pl.ANY)],
        out_specs=pl.BlockSpec((1, ds, n_heads * V_HEAD), lambda b, g, pt: (b, 0, 0)),
        scratch_shapes=[pltpu.VMEM((2, gp) + cache_ckv.shape[2:], cache_ckv.dtype),
                        pltpu.VMEM((2, gp) + cache_krope_t.shape[2:], cache_krope_t.dtype),
                        pltpu.SemaphoreType.DMA((2,)),
                        pltpu.VMEM((n_heads * ds, LANES), F32), pltpu.VMEM((n_heads * ds, LANES), F32),
                        pltpu.VMEM((n_heads * ds, KV_LORA), F32)],
    )
    return pl.pallas_call(
        functools.partial(_attn_sample_kernel, n_pages_step=gp, layer=layer),
        grid_spec=grid_spec,
        out_shape=jax.ShapeDtypeStruct((db, ds, n_heads * V_HEAD), BF16),
        compiler_params=_params("arbitrary", "arbitrary"),
        name="attention_sample",
    )(page_table, qf, kf_new, w_uv_pairs, cache_ckv, cache_krope_t)


def _merge_kernel(oa_ref, ob_ref, ga_ref, gb_ref, wa_ref, wb_ref, o_ref):
    ua = _dot(oa_ref[...], wa_ref[...])
    ub = _dot(ob_ref[...], wb_ref[...])
    o_ref[...] = (_sigmoid(ga_ref[...]) * ua + _sigmoid(gb_ref[...]) * ub).astype(o_ref.dtype)


def merge(o_a, o_b, p, ga_off, gb_off, w_up_a, w_up_b):
    n, d_a = o_a.shape
    d = w_up_a.shape[1]
    tm = _pick(n, 1376)
    tn = SEG
    return pl.pallas_call(
        _merge_kernel,
        grid=(n // tm, d // tn),
        in_specs=[pl.BlockSpec((tm, d_a), lambda i, j: (i, 0)),
                  pl.BlockSpec((tm, o_b.shape[1]), lambda i, j: (i, 0)),
                  pl.BlockSpec((tm, tn), lambda i, j: (i, ga_off // tn + j)),
                  pl.BlockSpec((tm, tn), lambda i, j: (i, gb_off // tn + j)),
                  pl.BlockSpec((d_a, tn), lambda i, j: (0, j)),
                  pl.BlockSpec((w_up_b.shape[0], tn), lambda i, j: (0, j))],
        out_specs=pl.BlockSpec((tm, tn), lambda i, j: (i, j)),
        out_shape=jax.ShapeDtypeStruct((n, d), BF16),
        compiler_params=_params("parallel", "arbitrary"),
        name="merge",
    )(o_a, o_b, p, p, w_up_a, w_up_b)


def _wo_kernel(m_ref, x_ref, w_ref, o_ref):
    o_ref[...] = x_ref[...] + _dot(m_ref[...], w_ref[...])


def out_proj(m, x, w_o):
    n, d = x.shape
    tm = _pick(n, 1376)
    tn = SEG
    return pl.pallas_call(
        _wo_kernel,
        grid=(n // tm, d // tn),
        in_specs=[pl.BlockSpec((tm, d), lambda i, j: (i, 0)),
                  pl.BlockSpec((tm, tn), lambda i, j: (i, j)),
                  pl.BlockSpec((d, tn), lambda i, j: (0, j))],
        out_specs=pl.BlockSpec((tm, tn), lambda i, j: (i, j)),
        out_shape=jax.ShapeDtypeStruct((n, d), F32),
        compiler_params=_params("parallel", "arbitrary"),
        name="out_proj",
    )(m, x, w_o)


def _router_kernel(x_ref, g_ref, w_ref, bias_ref, h_ref, r_ref):
    x = x_ref[...]
    h = x * lax.rsqrt(jnp.mean(x * x, -1, keepdims=True) + NORM_EPS) * g_ref[...]
    h_ref[...] = h.astype(h_ref.dtype)
    logits = _dot(h, w_ref[...], precision=HIGHEST)
    biased = logits + bias_ref[...]
    lane_i = lax.broadcasted_iota(jnp.int32, logits.shape, 1)
    lane = lane_i.astype(F32)
    lane_group = (lane_i // EXPERTS_PER_GROUP).astype(F32)
    is_g = (lane_i >= N_EXPERTS) & (lane_i < N_EXPERTS + N_GROUPS)
    big = float(1 << 20)

    def first_argmax(vals):
        mx = jnp.max(vals, axis=-1, keepdims=True)
        return jnp.min(jnp.where(vals == mx, lane, big), axis=-1, keepdims=True)

    def pick(vals, idx):
        return jnp.sum(jnp.where(lane == idx, vals, 0.0), axis=-1, keepdims=True)

    g_lane = first_argmax(jnp.where(is_g, biased, -jnp.inf))
    g_sel = g_lane - float(N_EXPERTS)
    g_max = jnp.max(jnp.where(is_g, logits, -jnp.inf), axis=-1, keepdims=True)
    g_exp = jnp.where(is_g, jnp.exp(logits - g_max), 0.0)
    p_sel = pick(g_exp, g_lane) / jnp.sum(g_exp, axis=-1, keepdims=True)

    in_grp = (lane_i < N_EXPERTS) & (lane_group == g_sel)
    e_biased = jnp.where(in_grp, biased, -jnp.inf)
    i1 = first_argmax(e_biased)
    i2 = first_argmax(jnp.where(lane == i1, -jnp.inf, e_biased))
    l1, l2 = pick(logits, i1), pick(logits, i2)
    mx = jnp.maximum(l1, l2)
    e1, e2 = jnp.exp(l1 - mx), jnp.exp(l2 - mx)
    w1 = e1 / (e1 + e2) * p_sel
    w2 = e2 / (e1 + e2) * p_sel
    r_ref[...] = jnp.where(lane_i == 0, i1, jnp.where(lane_i == 1, i2, jnp.where(lane_i == 2, w1,
                                                                                   jnp.where(lane_i == 3, w2, 0.0))))


def router(x, g, w_router, bias_router):
    n, d = x.shape
    tm = _pick(n, 688)
    return pl.pallas_call(
        _router_kernel,
        grid=(n // tm,),
        in_specs=[pl.BlockSpec((tm, d), lambda i: (i, 0)),
                  pl.BlockSpec((1, d), lambda i: (0, 0)),
                  pl.BlockSpec((d, LANES), lambda i: (0, 0)),
                  pl.BlockSpec((1, LANES), lambda i: (0, 0))],
        out_specs=[pl.BlockSpec((tm, d), lambda i: (i, 0)), pl.BlockSpec((tm, LANES), lambda i: (i, 0))],
        out_shape=[jax.ShapeDtypeStruct((n, d), BF16), jax.ShapeDtypeStruct((n, LANES), F32)],
        compiler_params=_params("parallel"),
        name="router",
    )(x, g, w_router, bias_router)


def _expert_kernel(te_ref, nt_ref, x_ref, gate_ref, wg_ref, wu_ref, wd_ref, o_ref, wg_bf, wu_bf, wd_bf):
    i = pl.program_id(0)

    @pl.when((i == 0) | (te_ref[i] != te_ref[jnp.maximum(i - 1, 0)]))
    def _():
        wg_bf[...] = wg_ref[...].astype(BF16)
        wu_bf[...] = wu_ref[...].astype(BF16)
        wd_bf[...] = wd_ref[...].astype(BF16)

    @pl.when(i < nt_ref[0])
    def _():
        x = x_ref[...]
        hg = _dot(x, wg_bf[...])
        hu = _dot(x, wu_bf[...])
        act = hg * _sigmoid(hg) * hu * gate_ref[...]
        o_ref[...] = _dot(act.astype(BF16), wd_bf[...])

    @pl.when(pl.program_id(0) >= nt_ref[0])
    def _():
        o_ref[...] = jnp.zeros(o_ref.shape, o_ref.dtype)


def experts(tile_expert, n_tiles_used, xg, gates, w_gate, w_up, w_down, layer):
    m_pad, d = xg.shape
    ff = w_gate.shape[-1]
    tm = EXPERT_TILE
    grid_spec = pltpu.PrefetchScalarGridSpec(
        num_scalar_prefetch=2,
        grid=(m_pad // tm,),
        in_specs=[pl.BlockSpec((tm, d), lambda i, te, nt: (i, 0)),
                  pl.BlockSpec((tm, 1), lambda i, te, nt: (i, 0)),
                  pl.BlockSpec((None, None, d, ff), lambda i, te, nt: (layer, te[i], 0, 0)),
                  pl.BlockSpec((None, None, d, ff), lambda i, te, nt: (layer, te[i], 0, 0)),
                  pl.BlockSpec((None, None, ff, d), lambda i, te, nt: (layer, te[i], 0, 0))],
        out_specs=pl.BlockSpec((tm, d), lambda i, te, nt: (i, 0)),
        scratch_shapes=[pltpu.VMEM((d, ff), BF16), pltpu.VMEM((d, ff), BF16), pltpu.VMEM((ff, d), BF16)],
    )
    return pl.pallas_call(
        _expert_kernel,
        grid_spec=grid_spec,
        out_shape=jax.ShapeDtypeStruct((m_pad, d), F32),
        compiler_params=_params("arbitrary"),
        name="experts",
    )(tile_expert, n_tiles_used, xg, gates, w_gate, w_up, w_down)


def _combine_kernel(x_ref, e0_ref, e1_ref, g_ref, o_ref):
    x = x_ref[...] + (e0_ref[...] + e1_ref[...])
    o_ref[...] = x * lax.rsqrt(jnp.mean(x * x, -1, keepdims=True) + NORM_EPS) * g_ref[...]


def combine(x, e0, e1, g, row_start):
    n, d = x.shape
    tm = _pick(int(np.gcd(n, row_start)) if row_start else n, 688)
    off = row_start // tm
    row = pl.BlockSpec((tm, d), lambda i: (i, 0))
    shifted = pl.BlockSpec((tm, d), lambda i: (i + off, 0))
    return pl.pallas_call(
        _combine_kernel,
        grid=(n // tm,),
        in_specs=[row, shifted, shifted, pl.BlockSpec((1, d), lambda i: (0, 0))],
        out_specs=row,
        out_shape=jax.ShapeDtypeStruct((n, d), F32),
        compiler_params=_params("parallel"),
        name="combine",
    )(x, e0, e1, g)


def _pad_cols(w, width):
    return jnp.pad(w, ((0, 0), (0, width - w.shape[1])))


def _pad_rows(w, height):
    return jnp.pad(w, ((0, height - w.shape[0]), (0, 0)))


def _swap_halves(w):
    half = w.shape[-1] // 2
    return jnp.concatenate([w[..., half:], w[..., :half]], axis=-1)


def _block_diag_pairs(w):
    h, a, b = w.shape
    z = jnp.zeros((h // 2, a, b), w.dtype)
    top = jnp.concatenate([w[0::2], z], axis=-1)
    bot = jnp.concatenate([z, w[1::2]], axis=-1)
    return jnp.concatenate([top, bot], axis=1)


def _rope_tables(pos):
    half = QK_ROPE // 2
    inv = ROPE_THETA ** (-jnp.arange(half, dtype=F32) / half)
    ang = pos.astype(F32)[:, None] * inv[None, :]
    cos, sin = jnp.cos(ang), jnp.sin(ang)
    cos_t = _pad_cols(jnp.concatenate([cos, cos], -1), LANES)
    sin_t = _pad_cols(jnp.concatenate([-sin, sin], -1), LANES)
    return cos_t, sin_t


def _layer_weights(l, w_in, rwkv_mu, rwkv_w0, rwkv_w2, rwkv_a0, rwkv_a2, rwkv_g2, rwkv_k_k, rwkv_k_a, rwkv_r_k,
                   rwkv_ln_g, rwkv_ln_b, mla_q_norm_g, mla_w_uq, mla_kv_norm_g, mla_w_uk, mla_w_uv, w_up_a,
                   w_up_b, w_o, norm_mix_g):
    d = w_in.shape[1]
    d_a = rwkv_w0.shape[1]
    n_heads = mla_w_uk.shape[2]
    off_kv, off_kr = Q_LORA, Q_LORA + KV_LORA
    off_rw = off_kr + QK_ROPE
    off_ga = off_rw + 3 * d_a + DECAY_LORA + AAA_LORA + GATE_LORA
    off_gb = off_ga + d
    w = w_in[l]
    rw = w[:, off_rw:off_ga]

    def lora_layout(m):
        wl = m[:, 3 * d_a:3 * d_a + DECAY_LORA]
        al = m[:, 3 * d_a + DECAY_LORA:3 * d_a + DECAY_LORA + AAA_LORA]
        gl = m[:, 3 * d_a + DECAY_LORA + AAA_LORA:]
        return jnp.concatenate([_pad_cols(wl, LANES), _pad_cols(al, LANES), gl], axis=1)

    kr = w[:, off_kr:off_rw]
    kv_seg = jnp.concatenate([w[:, off_kv:off_kr], _pad_cols(kr, LANES), _pad_cols(_swap_halves(kr), LANES)], axis=1)
    w_all = jnp.concatenate([rw[:, :3 * d_a], lora_layout(rw), w[:, off_ga:off_gb], w[:, off_gb:],
                             w[:, :Q_LORA], kv_seg], axis=1).astype(BF16)
    mu = rwkv_mu[l][None]
    mu_all = jnp.concatenate([mu[:, :3 * d_a], lora_layout(mu)], axis=1)

    uq = mla_w_uq[l].reshape(Q_LORA, n_heads, QK_NOPE + QK_ROPE)
    w_nope = uq[:, :, :QK_NOPE].reshape(Q_LORA, n_heads * QK_NOPE).astype(BF16)
    rope_cols = uq[:, :, QK_NOPE:]
    pad_heads = lambda m: jnp.pad(m, ((0, 0), (0, 0), (0, LANES - QK_ROPE))).reshape(Q_LORA, n_heads * LANES)
    w_rope = pad_heads(rope_cols).astype(BF16)
    w_rope_sw = pad_heads(_swap_halves(rope_cols)).astype(BF16)
    w_uk_pairs = _block_diag_pairs(jnp.transpose(mla_w_uk[l], (1, 2, 0))).astype(BF16)
    w_uv_pairs = _block_diag_pairs(jnp.transpose(mla_w_uv[l], (1, 0, 2))).astype(BF16)

    return dict(
        w_all=w_all, mu_all=mu_all, norm_mix_g=norm_mix_g[l][None],
        ga_off=3 * d_a + SEG, gb_off=3 * d_a + SEG + d,
        w0=rwkv_w0[l][None], a0=rwkv_a0[l][None], k_k=rwkv_k_k[l][None], k_a=rwkv_k_a[l][None],
        w2=_pad_rows(rwkv_w2[l], LANES).astype(BF16), a2=_pad_rows(rwkv_a2[l], LANES).astype(BF16),
        g2=rwkv_g2[l].astype(BF16),
        r_k=rwkv_r_k[l].reshape(1, d_a), ln_g=rwkv_ln_g[l][None], ln_b=rwkv_ln_b[l][None],
        g_q=mla_q_norm_g[l][None], g_kv=mla_kv_norm_g[l][None],
        w_nope=w_nope, w_rope=w_rope, w_rope_sw=w_rope_sw, w_uk_pairs=w_uk_pairs, w_uv_pairs=w_uv_pairs,
        w_up_a=w_up_a[l].astype(BF16), w_up_b=w_up_b[l].astype(BF16), w_o=w_o[l].astype(BF16),
        n_heads=n_heads, d_a=d_a, lora_layout=lora_layout,
    )


def _token_mixer(x3, pos, shift_prev, wkv_prev, lw, attend):
    b, t, d = x3.shape
    n = b * t
    d_a = lw["d_a"]
    x2 = x3.reshape(n, d)
    p = rms_matmul(x2, lw["norm_mix_g"], lw["w_all"])
    p3 = p.reshape(b, t, p.shape[1])

    chunk = _pick(t, SCAN_CHUNK)
    if t >= 344:
        bb, tt = 1, _pick(t, 344)
        tt_prep = _pick(t, 688, chunk)
    else:
        bb, tt = _pick(b, max(1, 256 // t), 1), t
        tt_prep = t

    def column_tiles(m):
        n_ct = d_a // SEG
        parts = [m[:, part * d_a:(part + 1) * d_a].reshape(-1, n_ct, SEG) for part in range(3)]
        lora = jnp.broadcast_to(m[:, None, 3 * d_a:], (m.shape[0], n_ct, SEG))
        return jnp.concatenate(parts + [lora], axis=-1)[:, :, None, :]

    shift_cols = jnp.concatenate([shift_prev[:, :3 * d_a], lw["lora_layout"](shift_prev)], axis=1)
    rt, qu, kh, bh, kt, bt, v, gam, bonus, g = rwkv_prepare(
        p3, column_tiles(shift_cols), column_tiles(lw["mu_all"])[0], lw["w0"], lw["a0"], lw["k_k"], lw["k_a"],
        lw["r_k"], lw["w2"], lw["a2"], lw["g2"], bb, tt_prep, chunk)
    y, s_new = rwkv_scan(rt, qu, kh, bh, kt, bt, v, gam, wkv_prev, chunk)
    flat = lambda a: a.reshape(n, d_a)
    o_a = rwkv_out(flat(y), flat(bonus), flat(g), lw["ln_g"], lw["ln_b"])

    cos_t, sin_t = _rope_tables(pos)
    qf, kf, ckv, krope = mla_project(p3, cos_t, sin_t, lw["g_q"], lw["g_kv"], lw["w_nope"], lw["w_rope"],
                                     lw["w_rope_sw"], lw["w_uk_pairs"], lw["n_heads"], bb, tt)
    o_b = attend(qf, kf).reshape(n, -1)

    m = merge(o_a, o_b, p, lw["ga_off"], lw["gb_off"], lw["w_up_a"], lw["w_up_b"])
    x_new = out_proj(m, x2, lw["w_o"])

    last = p3[:, -1, :]
    shift_new = jnp.concatenate([last[:, :3 * d_a],
                                 last[:, 3 * d_a:3 * d_a + DECAY_LORA],
                                 last[:, 3 * d_a + LANES:3 * d_a + LANES + AAA_LORA],
                                 last[:, 3 * d_a + 2 * LANES:3 * d_a + SEG]], axis=1)
    return x_new, (ckv, krope, s_new, shift_new)


def _moe(x_rows, norm_g, router_group_w, router_group_b, router_expert_w, router_expert_b,
         w_gate, w_up, w_down, layer, final_g):
    w_router = _pad_cols(jnp.concatenate([router_expert_w, router_group_w], axis=1), LANES)
    b_router = _pad_cols(jnp.concatenate([router_expert_b, router_group_b])[None], LANES)
    hs, routes = zip(*[router(x, norm_g, w_router, b_router) for x in x_rows])
    h = jnp.concatenate(hs, axis=0)
    route = jnp.concatenate(routes, axis=0)
    n = h.shape[0]
    eid = route[:, :2].astype(jnp.int32).reshape(-1)
    gate = route[:, 2:4].reshape(-1)

    tm = EXPERT_TILE
    n_tiles = -(-(2 * n + N_EXPERTS * (tm - 1)) // tm)
    m_pad = n_tiles * tm
    experts_iota = jnp.arange(N_EXPERTS, dtype=jnp.int32)
    order = jnp.argsort(eid, stable=True).astype(jnp.int32)
    rank = jnp.argsort(order).astype(jnp.int32)
    counts = jnp.sum((eid[:, None] == experts_iota[None, :]).astype(jnp.int32), axis=0)
    padded = ((counts + tm - 1) // tm) * tm
    seg_end = jnp.cumsum(padded)
    seg_start = seg_end - padded
    cnt_start = jnp.cumsum(counts) - counts
    dest = jnp.take(seg_start - cnt_start, eid, mode="clip") + rank
    tile_start = jnp.arange(n_tiles, dtype=jnp.int32) * tm
    tile_expert = jnp.minimum(jnp.sum((seg_end[None, :] <= tile_start[:, None]).astype(jnp.int32), axis=1),
                              N_EXPERTS - 1)
    n_used = (seg_end[-1] // tm).astype(jnp.int32).reshape(1)
    row_expert = jnp.repeat(tile_expert, tm)
    within = jnp.arange(m_pad, dtype=jnp.int32) - jnp.take(seg_start, row_expert, mode="clip")
    valid = within < jnp.take(counts, row_expert, mode="clip")
    row_assign = jnp.take(order, jnp.take(cnt_start, row_expert, mode="clip") + within, mode="clip")
    row_token = jnp.where(valid, row_assign // 2, 0)
    row_gate = jnp.where(valid, jnp.take(gate, row_assign, mode="clip"), 0.0)

    xg = jnp.take(h, row_token, axis=0, mode="clip")
    eo = experts(tile_expert, n_used, xg, row_gate[:, None], w_gate, w_up, w_down, layer)
    dest2 = dest.reshape(n, 2)
    e0 = jnp.take(eo, dest2[:, 0], axis=0, mode="clip")
    e1 = jnp.take(eo, dest2[:, 1], axis=0, mode="clip")

    outs, start = [], 0
    for x in x_rows:
        outs.append(combine(x, e0, e1, final_g, start))
        start += x.shape[0]
    return outs


def kernel(x_prompt, x_sample, cache_ckv, cache_krope, state_wkv, state_shift, page_table, meta_tokens, norm_mix_g, w_in, rwkv_mu, rwkv_w0, rwkv_w2, rwkv_a0, rwkv_a2, rwkv_g2, rwkv_k_k, rwkv_k_a, rwkv_r_k, rwkv_ln_g, rwkv_ln_b, mla_q_norm_g, mla_w_uq, mla_kv_norm_g, mla_w_uk, mla_w_uv, w_up_a, w_up_b, w_o, norm_ffn_g, router_group_w, router_group_b, router_expert_w, router_expert_b, expert_w_gate, expert_w_up, expert_w_down, norm_final_g):
    depth = w_in.shape[0]
    assert depth == 1, "the final norm is fused into the MoE combine, which assumes a single layer"
    b, s, d = x_prompt.shape
    db, ds, _ = x_sample.shape
    past = page_table.shape[1] * PAGE_SIZE
    t = N_META + s
    n_heads_a = state_wkv.shape[2]
    rwkv_cols = state_shift.shape[-1]

    xp = jnp.concatenate([jnp.broadcast_to(meta_tokens[None], (b, N_META, d)), x_prompt], axis=1)
    l = 0
    lw = _layer_weights(l, w_in, rwkv_mu, rwkv_w0, rwkv_w2, rwkv_a0, rwkv_a2, rwkv_g2, rwkv_k_k, rwkv_k_a,
                        rwkv_r_k, rwkv_ln_g, rwkv_ln_b, mla_q_norm_g, mla_w_uq, mla_kv_norm_g, mla_w_uk,
                        mla_w_uv, w_up_a, w_up_b, w_o, norm_mix_g)

    xp1, st_p = _token_mixer(xp, jnp.arange(t), jnp.zeros((b, rwkv_cols), F32),
                             jnp.zeros((b, n_heads_a, HEAD_A, HEAD_A), F32), lw,
                             lambda qf, kf: attention_prompt(qf, kf, lw["w_uv_pairs"]))
    xs1, st_s = _token_mixer(x_sample, past + jnp.arange(ds), state_shift[l], state_wkv[l], lw,
                             lambda qf, kf: attention_sample(qf, kf, cache_ckv, jnp.swapaxes(cache_krope, 2, 3), l,
                                                             page_table, lw["w_uv_pairs"]))

    yp, ys = _moe([xp1, xs1], norm_ffn_g[l][None], router_group_w[l], router_group_b[l], router_expert_w[l],
                  router_expert_b[l], expert_w_gate, expert_w_up, expert_w_down, l, norm_final_g[None])
    y_prompt = yp.reshape(b, t, d)[:, N_META:]
    y_sample = ys.reshape(db, ds, d)
    return (y_prompt, y_sample,
            st_p[0][None], st_p[1][None], st_p[2][None], st_p[3][None],
            st_s[0][None], st_s[1][None], st_s[2][None], st_s[3][None])
```
